```python
import jax
import jax.numpy as jnp
from jax import lax
import numpy as np

D_MODEL = 1024
BATCH = 8
SEQ = 8192
DEPTH = 2
DEC_BATCH = 8
DEC_SEQ = 64
PAST_LEN = 4096

CHUNK = 64
N_AB_LAYERS = (DEPTH + 1) // 2
N_C_LAYERS = DEPTH // 2
D_REC = D_MODEL // 2
REC_HEADS = 8
REC_BLOCK = D_REC // REC_HEADS
REC_CONV = 4
LRU_C = 8.0
D_CONV = D_MODEL // 2
CONV_WIDTH = 31
N_HEADS = 16
N_KV = 4
GROUP = N_HEADS // N_KV
HEAD_DIM = 64
WINDOW = 128
WIN_CHUNKS = WINDOW // CHUNK
D_FF = ((8 * D_MODEL // 3 + 255) // 256) * 256
ALPHA = (2 * DEPTH) ** 0.25
BETA = (8 * DEPTH) ** -0.25
LN_EPS = 1e-5
NEG_INF = -1e30
D_IN_AB = 2 * D_REC + 2 * D_CONV
D_Q = N_HEADS * HEAD_DIM
D_KV = N_KV * HEAD_DIM

kernel_name = 'hybrid_rglru_conformer_swa_step'


def layer_norm(x, g, b):
    xf = x.astype(jnp.float32)
    mu = xf.mean(-1, keepdims=True)
    var = jnp.mean(jnp.square(xf - mu), -1, keepdims=True)
    return ((xf - mu) * lax.rsqrt(var + LN_EPS) * g.astype(jnp.float32) + b.astype(jnp.float32)).astype(x.dtype)


def causal_dwconv(x, buf, w, b):
    k = w.shape[0]
    xp = jnp.concatenate([buf.astype(x.dtype), x], axis=1)
    y = lax.conv_general_dilated(xp, w[:, None, :].astype(x.dtype), window_strides=(1,), padding='VALID',
                                 dimension_numbers=('NWC', 'WIO', 'NWC'), feature_group_count=x.shape[-1])
    return y + b.astype(x.dtype), xp[:, xp.shape[1] - (k - 1):]


def rg_lru(x, h0, ga_w, ga_b, gx_w, gx_b, lam):
    bsz, t, _ = x.shape
    xf = x.astype(jnp.float32)
    xb = xf.reshape(bsz, t, REC_HEADS, REC_BLOCK)
    r = jax.nn.sigmoid(jnp.einsum('bthi,hij->bthj', xb, ga_w.astype(jnp.float32)).reshape(bsz, t, D_REC) + ga_b.astype(jnp.float32))
    i = jax.nn.sigmoid(jnp.einsum('bthi,hij->bthj', xb, gx_w.astype(jnp.float32)).reshape(bsz, t, D_REC) + gx_b.astype(jnp.float32))
    log_a = -LRU_C * r * jax.nn.softplus(-lam.astype(jnp.float32))
    a = jnp.exp(log_a)
    u = jnp.sqrt(-jnp.expm1(2.0 * log_a)) * (i * xf)
    u = u.at[:, 0].add(a[:, 0] * h0.astype(jnp.float32))

    def combine(left, right):
        a1, b1 = left
        a2, b2 = right
        return a1 * a2, a2 * b1 + b2

    _, h = lax.associative_scan(combine, (a, u), axis=1)
    return h.astype(x.dtype), h[:, -1].astype(h0.dtype)


def mixer_ab(x, rc_buf, h0, cf_buf, w_in, rc_w, rc_b, ga_w, ga_b, gx_w, gx_b, lam, cf_w, cf_b, cf_g, cf_beta, w_out):
    u = x @ w_in
    xr, yr, cv, cg = jnp.split(u, [D_REC, 2 * D_REC, 2 * D_REC + D_CONV], axis=-1)
    xc, new_rc = causal_dwconv(xr, rc_buf, rc_w, rc_b)
    h, h_last = rg_lru(xc, h0, ga_w, ga_b, gx_w, gx_b, lam)
    rec_out = h * jax.nn.gelu(yr)
    g = cv * jax.nn.sigmoid(cg)
    c, new_cf = causal_dwconv(g, cf_buf, cf_w, cf_b)
    c = jax.nn.silu(layer_norm(c, cf_g, cf_beta))
    out = jnp.concatenate([rec_out, c], axis=-1) @ w_out
    return out, new_rc, h_last, new_cf


def qkv_proj(x, w_qkv):
    bsz, t, _ = x.shape
    u = x @ w_qkv
    q, k, v = jnp.split(u, [D_Q, D_Q + D_KV], axis=-1)
    q = q.reshape(bsz, t, N_KV, GROUP, HEAD_DIM) * (HEAD_DIM ** -0.5)
    return q, k.reshape(bsz, t, N_KV, HEAD_DIM), v.reshape(bsz, t, N_KV, HEAD_DIM)


def alibi_bias(dist):
    slopes = jnp.exp2(-8.0 * jnp.arange(1, N_HEADS + 1, dtype=jnp.float32) / N_HEADS).reshape(N_KV, GROUP)
    return -slopes[:, :, None, None] * jnp.abs(dist).astype(jnp.float32)[None, None]


def sink_attention(q, k, v, bias, valid, sinks):
    s = jnp.einsum('...qkgd,...skd->...kgqs', q, k, preferred_element_type=jnp.float32)
    s = jnp.where(valid[..., None, None, :, :], s + bias, NEG_INF)
    sink = sinks.astype(jnp.float32).reshape(N_KV, GROUP)[:, :, None]
    m = jnp.maximum(s.max(-1), sink)
    p = jnp.exp(s - m[..., None])
    denom = p.sum(-1) + jnp.exp(sink - m)
    o = jnp.einsum('...kgqs,...skd->...qkgd', p, v.astype(jnp.float32))
    o = o / jnp.moveaxis(denom, -1, -3)[..., None]
    return o.astype(q.dtype)


def mixer_c_prompt(x, w_qkv, sinks, w_out):
    bsz, t, _ = x.shape
    nc = t // CHUNK
    q, k, v = qkv_proj(x, w_qkv)
    qc = q.reshape(bsz, nc, CHUNK, N_KV, GROUP, HEAD_DIM)

    def windows(z):
        zc = z.reshape(bsz, nc, CHUNK, N_KV, HEAD_DIM)
        zp = jnp.pad(zc, ((0, 0), (WIN_CHUNKS, 0), (0, 0), (0, 0), (0, 0)))
        return jnp.concatenate([zp[:, j:j + nc] for j in range(WIN_CHUNKS + 1)], axis=2)

    kw, vw = windows(k), windows(v)
    koff = jnp.arange((WIN_CHUNKS + 1) * CHUNK)
    dist = WIN_CHUNKS * CHUNK + jnp.arange(CHUNK)[:, None] - koff[None, :]
    bias = alibi_bias(dist)
    key_chunk = jnp.arange(nc)[:, None] - WIN_CHUNKS + (koff // CHUNK)[None, :]
    valid = (key_chunk >= 0)[:, None, :]
    o = lax.map(lambda a: sink_attention(a[0], a[1], a[2], bias, valid, sinks), (qc, kw, vw))
    out = o.reshape(bsz, t, D_Q) @ w_out
    return out, k[:, t - WINDOW:], v[:, t - WINDOW:]


def mixer_c_sample(x, ck, cv, w_qkv, sinks, w_out):
    bsz, t, _ = x.shape
    rows = ck.shape[1]
    q, k, v = qkv_proj(x, w_qkv)
    kf = jnp.concatenate([ck.astype(k.dtype), k], axis=1)
    vf = jnp.concatenate([cv.astype(v.dtype), v], axis=1)
    q_pos = PAST_LEN + jnp.arange(t)
    k_pos = jnp.concatenate([PAST_LEN - rows + jnp.arange(rows), q_pos])
    bias = alibi_bias(q_pos[:, None] - k_pos[None, :])
    qch, kch = q_pos[:, None] // CHUNK, k_pos[None, :] // CHUNK
    valid = (kch <= qch) & (kch >= qch - WIN_CHUNKS)
    o = sink_attention(q, kf, vf, bias, valid, sinks)
    out = o.reshape(bsz, t, D_Q) @ w_out
    return out, kf[:, kf.shape[1] - rows:], vf[:, vf.shape[1] - rows:]


def swiglu(x, wg, wu, wd):
    return (jax.nn.silu(x @ wg) * (x @ wu)) @ wd


def setup_inputs(seed: int = 0) -> dict:
    key = jax.random.key(seed)
    ks = iter(jax.random.split(key, 40))

    def nrm(shape, scale):
        return scale * jax.random.normal(next(ks), shape, jnp.float32)

    win_rows = min(WINDOW, PAST_LEN)
    ua = jax.random.uniform(next(ks), (N_AB_LAYERS, D_REC), jnp.float32, 0.9, 0.999)
    sig = ua ** (1.0 / LRU_C)
    rec_lambda = jnp.log(sig) - jnp.log1p(-sig)
    w_qk = nrm((N_C_LAYERS, D_MODEL, D_Q + D_KV), D_MODEL ** -0.5)
    w_v = nrm((N_C_LAYERS, D_MODEL, D_KV), BETA * D_MODEL ** -0.5)
    return {
        'x_prompt': nrm((BATCH, SEQ, D_MODEL), 1.0),
        'x_sample': nrm((DEC_BATCH, DEC_SEQ, D_MODEL), 1.0),
        'state_rec_h': nrm((N_AB_LAYERS, DEC_BATCH, D_REC), 0.5),
        'state_rec_conv': nrm((N_AB_LAYERS, DEC_BATCH, REC_CONV - 1, D_REC), 1.0),
        'state_cf_conv': nrm((N_AB_LAYERS, DEC_BATCH, CONV_WIDTH - 1, D_CONV), 1.0),
        'cache_k': nrm((N_C_LAYERS, DEC_BATCH, win_rows, N_KV, HEAD_DIM), 1.0),
        'cache_v': nrm((N_C_LAYERS, DEC_BATCH, win_rows, N_KV, HEAD_DIM), BETA),
        'w_in_ab': nrm((N_AB_LAYERS, D_MODEL, D_IN_AB), D_MODEL ** -0.5),
        'rec_conv_w': nrm((N_AB_LAYERS, REC_CONV, D_REC), REC_CONV ** -0.5),
        'rec_conv_b': nrm((N_AB_LAYERS, D_REC), 0.02),
        'rec_gate_a_w': nrm((N_AB_LAYERS, REC_HEADS, REC_BLOCK, REC_BLOCK), REC_BLOCK ** -0.5),
        'rec_gate_a_b': nrm((N_AB_LAYERS, D_REC), 0.02),
        'rec_gate_x_w': nrm((N_AB_LAYERS, REC_HEADS, REC_BLOCK, REC_BLOCK), REC_BLOCK ** -0.5),
        'rec_gate_x_b': nrm((N_AB_LAYERS, D_REC), 0.02),
        'rec_lambda': rec_lambda,
        'cf_conv_w': nrm((N_AB_LAYERS, CONV_WIDTH, D_CONV), CONV_WIDTH ** -0.5),
        'cf_conv_b': nrm((N_AB_LAYERS, D_CONV), 0.02),
        'cf_norm_g': 1.0 + nrm((N_AB_LAYERS, D_CONV), 0.02),
        'cf_norm_b': nrm((N_AB_LAYERS, D_CONV), 0.02),
        'w_out_ab': nrm((N_AB_LAYERS, D_REC + D_CONV, D_MODEL), BETA * (D_REC + D_CONV) ** -0.5),
        'w_qkv': jnp.concatenate([w_qk, w_v], axis=-1),
        'attn_sinks': nrm((N_C_LAYERS, N_HEADS), 0.5),
        'w_out_c': nrm((N_C_LAYERS, D_Q, D_MODEL), BETA * D_Q ** -0.5),
        'ln_mix_g': 1.0 + nrm((DEPTH, D_MODEL), 0.02),
        'ln_mix_b': nrm((DEPTH, D_MODEL), 0.02),
        'w_ff_gate': nrm((DEPTH, D_MODEL, D_FF), BETA * D_MODEL ** -0.5),
        'w_ff_up': nrm((DEPTH, D_MODEL, D_FF), BETA * D_MODEL ** -0.5),
        'w_ff_down': nrm((DEPTH, D_FF, D_MODEL), BETA * D_FF ** -0.5),
        'ln_ff_g': 1.0 + nrm((DEPTH, D_MODEL), 0.02),
        'ln_ff_b': nrm((DEPTH, D_MODEL), 0.02),
    }


def reference(x_prompt, x_sample, state_rec_h, state_rec_conv, state_cf_conv, cache_k, cache_v,
              w_in_ab, rec_conv_w, rec_conv_b, rec_gate_a_w, rec_gate_a_b, rec_gate_x_w, rec_gate_x_b,
              rec_lambda, cf_conv_w, cf_conv_b, cf_norm_g, cf_norm_b, w_out_ab,
              w_qkv, attn_sinks, w_out_c,
              ln_mix_g, ln_mix_b, w_ff_gate, w_ff_up, w_ff_down, ln_ff_g, ln_ff_b):
    yp, ys = x_prompt, x_sample
    p_h, s_h, p_rc, s_rc, p_cf, s_cf, p_k, s_k, p_v, s_v = ([] for _ in range(10))
    for layer in range(DEPTH):
        if layer % 2 == 0:
            j = layer // 2
            prm = (w_in_ab[j], rec_conv_w[j], rec_conv_b[j], rec_gate_a_w[j], rec_gate_a_b[j],
                   rec_gate_x_w[j], rec_gate_x_b[j], rec_lambda[j], cf_conv_w[j], cf_conv_b[j],
                   cf_norm_g[j], cf_norm_b[j], w_out_ab[j])
            nb = yp.shape[0]
            mp, rc, hl, cf = mixer_ab(yp, jnp.zeros((nb, REC_CONV - 1, D_REC), yp.dtype),
                                      jnp.zeros((nb, D_REC), yp.dtype),
                                      jnp.zeros((nb, CONV_WIDTH - 1, D_CONV), yp.dtype), *prm)
            p_rc.append(rc)
            p_h.append(hl)
            p_cf.append(cf)
            ms, rc, hl, cf = mixer_ab(ys, state_rec_conv[j], state_rec_h[j], state_cf_conv[j], *prm)
            s_rc.append(rc)
            s_h.append(hl)
            s_cf.append(cf)
        else:
            j = layer // 2
            mp, kk, vv = mixer_c_prompt(yp, w_qkv[j], attn_sinks[j], w_out_c[j])
            p_k.append(kk)
            p_v.append(vv)
            ms, kk, vv = mixer_c_sample(ys, cache_k[j], cache_v[j], w_qkv[j], attn_sinks[j], w_out_c[j])
            s_k.append(kk)
            s_v.append(vv)
        yp = layer_norm(ALPHA * yp + mp, ln_mix_g[layer], ln_mix_b[layer])
        ys = layer_norm(ALPHA * ys + ms, ln_mix_g[layer], ln_mix_b[layer])
        yp = layer_norm(ALPHA * yp + swiglu(yp, w_ff_gate[layer], w_ff_up[layer], w_ff_down[layer]), ln_ff_g[layer], ln_ff_b[layer])
        ys = layer_norm(ALPHA * ys + swiglu(ys, w_ff_gate[layer], w_ff_up[layer], w_ff_down[layer]), ln_ff_g[layer], ln_ff_b[layer])
    return (yp, ys, jnp.stack(p_h), jnp.stack(s_h), jnp.stack(p_rc), jnp.stack(s_rc),
            jnp.stack(p_cf), jnp.stack(s_cf), jnp.stack(p_k), jnp.stack(s_k), jnp.stack(p_v), jnp.stack(s_v))
```

```python
import functools

import jax
import jax.numpy as jnp
from jax import lax
from jax.experimental import pallas as pl
from jax.experimental.pallas import tpu as pltpu

CHUNK = 64
WINDOW = 128
HEAD_DIM = 64
N_HEADS = 16
N_KV = 4
GROUP = N_HEADS // N_KV
REC_CONV = 4
CONV_WIDTH = 31
LRU_C = 8.0
LN_EPS = 1e-5
NEG_INF = -1e30

SUBLANES = 8
MXU_DIM = 256
VMEM_LIMIT_BYTES = 56 * 1024 * 1024

BF16 = jnp.bfloat16
F32 = jnp.float32


def _dot(a, b):
    return jnp.dot(a, b, preferred_element_type=F32)


def _layer_norm(z, g, b):
    mu = jnp.mean(z, axis=-1, keepdims=True)
    zc = z - mu
    var = jnp.mean(zc * zc, axis=-1, keepdims=True)
    return zc * lax.rsqrt(var + LN_EPS) * g + b


def _sigmoid(x):
    return jax.nn.sigmoid(x)


def _mixer_ab_body(alpha, x_ref, rc0_ref, h0_ref, cf0_ref, w_in_ref, rcw_ref, rcb_ref, wgate_ref,
                   gab_ref, gxb_ref, lam_ref, cfw_ref, cfb_ref, cfg_ref, cfbeta_ref, w_out_ref,
                   lng_ref, lnb_ref,
                   y_ref, rc_out_ref, h_out_ref, cf_out_ref,
                   xr_buf, g_buf, a_buf, b_buf, h_buf, hcar):
    tt = x_ref.shape[0]
    d_rec = rcw_ref.shape[1]
    d_conv = cfw_ref.shape[1]
    rc_tail = REC_CONV - 1
    cf_tail = CONV_WIDTH - 1
    rc_base = SUBLANES - rc_tail
    cf_base = 32 - cf_tail
    t = pl.program_id(1)

    @pl.when(t == 0)
    def _():
        xr_buf[rc_base:SUBLANES, :] = rc0_ref[...]
        g_buf[cf_base:32, :] = cf0_ref[...]
        hcar[...] = jnp.broadcast_to(h0_ref[...], hcar.shape)

    x = x_ref[...]
    xb = x.astype(BF16)

    xr = _dot(xb, w_in_ref[:, 0:d_rec])
    xr_buf[SUBLANES:SUBLANES + tt, :] = xr
    xc = rcb_ref[...] + rcw_ref[rc_tail:rc_tail + 1, :] * xr
    for k in range(rc_tail):
        xc = xc + rcw_ref[k:k + 1, :] * xr_buf[rc_base + k:rc_base + k + tt, :]
    xcb = xc.astype(BF16)
    r_parts, i_parts = [], []
    for j in range(d_rec // MXU_DIM):
        lo, hi = j * MXU_DIM, (j + 1) * MXU_DIM
        gj = _dot(xcb[:, lo:hi], wgate_ref[j])
        r_parts.append(_sigmoid(gj[:, :MXU_DIM] + gab_ref[:, lo:hi]))
        i_parts.append(_sigmoid(gj[:, MXU_DIM:] + gxb_ref[:, lo:hi]))
    r = jnp.concatenate(r_parts, axis=1)
    i = jnp.concatenate(i_parts, axis=1)
    nlam = -lam_ref[...]
    softplus = jnp.maximum(nlam, 0.0) + jnp.log1p(jnp.exp(-jnp.abs(nlam)))
    log_a = (-LRU_C * softplus) * r
    a = jnp.exp(log_a)
    bu = jnp.sqrt(1.0 - a * a) * (i * xc)

    groups = tt // SUBLANES
    a3 = a.reshape(groups, SUBLANES, d_rec)
    b3 = bu.reshape(groups, SUBLANES, d_rec)
    sub = lax.broadcasted_iota(jnp.int32, a3.shape, 1)
    d = 1
    while d < SUBLANES:
        keep = sub >= d
        a_prev = jnp.where(keep, pltpu.roll(a3, d, axis=1), 1.0)
        b_prev = jnp.where(keep, pltpu.roll(b3, d, axis=1), 0.0)
        b3 = a3 * b_prev + b3
        a3 = a3 * a_prev
        d *= 2
    a_buf[...] = a3.reshape(tt, d_rec)
    b_buf[...] = b3.reshape(tt, d_rec)

    h = hcar[...]
    for j in range(groups):
        rows = slice(j * SUBLANES, (j + 1) * SUBLANES)
        hj = a_buf[rows, :] * h + b_buf[rows, :]
        h_buf[rows, :] = hj
        h = jnp.broadcast_to(hj[SUBLANES - 1:SUBLANES, :], hj.shape)
    hcar[...] = h

    yr = _dot(xb, w_in_ref[:, d_rec:2 * d_rec])
    rec_out = h_buf[...] * jax.nn.gelu(yr)

    cv = _dot(xb, w_in_ref[:, 2 * d_rec:2 * d_rec + d_conv])
    cg = _dot(xb, w_in_ref[:, 2 * d_rec + d_conv:2 * d_rec + 2 * d_conv])
    gl = cv * _sigmoid(cg)
    g_buf[32:32 + tt, :] = gl
    c = cfb_ref[...] + cfw_ref[cf_tail:cf_tail + 1, :] * gl
    for k in range(cf_tail):
        c = c + cfw_ref[k:k + 1, :] * g_buf[cf_base + k:cf_base + k + tt, :]
    cn = _layer_norm(c, cfg_ref[...], cfbeta_ref[...])
    c2 = cn * _sigmoid(cn)

    out = _dot(rec_out.astype(BF16), w_out_ref[0:d_rec, :]) + _dot(c2.astype(BF16), w_out_ref[d_rec:d_rec + d_conv, :])
    y_ref[...] = _layer_norm(alpha * x + out, lng_ref[...], lnb_ref[...])

    new_rc = xr_buf[SUBLANES + tt - rc_tail:SUBLANES + tt, :]
    new_cf = g_buf[32 + tt - cf_tail:32 + tt, :]
    xr_buf[rc_base:SUBLANES, :] = new_rc
    g_buf[cf_base:32, :] = new_cf
    rc_out_ref[...] = new_rc
    cf_out_ref[...] = new_cf
    h_out_ref[...] = h[0:1, :]


def _const_spec(arr):
    zeros = (0,) * arr.ndim
    return pl.BlockSpec(arr.shape, lambda b, t: zeros)


def _mixer_ab_call(x, rc0, h0, cf0, prm, ln_g, ln_b, alpha, tt):
    bsz, seq, d_model = x.shape
    (w_in, rcw, rcb, wgate, gab, gxb, lam, cfw, cfb, cfg, cfbeta, w_out) = prm
    d_rec, d_conv = rcw.shape[1], cfw.shape[1]
    assert seq % tt == 0 and tt % SUBLANES == 0 and tt >= CONV_WIDTH - 1
    grid = (bsz, seq // tt)
    h0 = h0.reshape(bsz, 1, d_rec)
    consts = (w_in, rcw, rcb, wgate, gab, gxb, lam, cfw, cfb, cfg, cfbeta, w_out, ln_g, ln_b)
    in_specs = [
        pl.BlockSpec((None, tt, d_model), lambda b, t: (b, t, 0)),
        pl.BlockSpec((None, REC_CONV - 1, d_rec), lambda b, t: (b, 0, 0)),
        pl.BlockSpec((None, 1, d_rec), lambda b, t: (b, 0, 0)),
        pl.BlockSpec((None, CONV_WIDTH - 1, d_conv), lambda b, t: (b, 0, 0)),
    ] + [_const_spec(c) for c in consts]
    out_shape = (
        jax.ShapeDtypeStruct((bsz, seq, d_model), F32),
        jax.ShapeDtypeStruct((bsz, REC_CONV - 1, d_rec), F32),
        jax.ShapeDtypeStruct((bsz, 1, d_rec), F32),
        jax.ShapeDtypeStruct((bsz, CONV_WIDTH - 1, d_conv), F32),
    )
    out_specs = (
        pl.BlockSpec((None, tt, d_model), lambda b, t: (b, t, 0)),
        pl.BlockSpec((None, REC_CONV - 1, d_rec), lambda b, t: (b, 0, 0)),
        pl.BlockSpec((None, 1, d_rec), lambda b, t: (b, 0, 0)),
        pl.BlockSpec((None, CONV_WIDTH - 1, d_conv), lambda b, t: (b, 0, 0)),
    )
    scratch = [
        pltpu.VMEM((tt + SUBLANES, d_rec), F32),
        pltpu.VMEM((tt + 32, d_conv), F32),
        pltpu.VMEM((tt, d_rec), F32),
        pltpu.VMEM((tt, d_rec), F32),
        pltpu.VMEM((tt, d_rec), F32),
        pltpu.VMEM((SUBLANES, d_rec), F32),
    ]
    y, rc, h, cf = pl.pallas_call(
        functools.partial(_mixer_ab_body, alpha),
        grid=grid, in_specs=in_specs, out_specs=out_specs, out_shape=out_shape,
        scratch_shapes=scratch,
        compiler_params=pltpu.CompilerParams(
            dimension_semantics=("arbitrary", "arbitrary"), vmem_limit_bytes=VMEM_LIMIT_BYTES),
        name="mixer_ab",
    )(x, rc0, h0, cf0, *consts)
    return y, rc, h.reshape(bsz, d_rec), cf


def _mixer_c_body(alpha, masked, sink_ref, x_ref, k0_ref, v0_ref, wqkv_ref, w_out_ref, lng_ref, lnb_ref,
                  y_ref, k_out_ref, v_out_ref,
                  q_buf, k_buf, v_buf, o_buf):
    tt = x_ref.shape[0]
    d_q = N_HEADS * HEAD_DIM
    d_kv = N_KV * HEAD_DIM
    win_keys = WINDOW + CHUNK
    t = pl.program_id(1)

    @pl.when(t == 0)
    def _():
        k_buf[0:WINDOW, :] = k0_ref[...].astype(BF16)
        v_buf[0:WINDOW, :] = v0_ref[...].astype(BF16)

    x = x_ref[...]
    xb = x.astype(BF16)
    q_buf[...] = (_dot(xb, wqkv_ref[:, 0:d_q]) * (HEAD_DIM ** -0.5)).astype(BF16)
    k = _dot(xb, wqkv_ref[:, d_q:d_q + d_kv])
    v = _dot(xb, wqkv_ref[:, d_q + d_kv:d_q + 2 * d_kv])
    k_buf[WINDOW:WINDOW + tt, :] = k.astype(BF16)
    v_buf[WINDOW:WINDOW + tt, :] = v.astype(BF16)
    if tt >= WINDOW:
        k_out_ref[...] = k[tt - WINDOW:tt, :]
        v_out_ref[...] = v[tt - WINDOW:tt, :]
    else:
        k_out_ref[0:WINDOW - tt, :] = k0_ref[tt:WINDOW, :]
        v_out_ref[0:WINDOW - tt, :] = v0_ref[tt:WINDOW, :]
        k_out_ref[WINDOW - tt:WINDOW, :] = k
        v_out_ref[WINDOW - tt:WINDOW, :] = v

    qi = lax.broadcasted_iota(jnp.int32, (CHUNK, win_keys), 0)
    ko = lax.broadcasted_iota(jnp.int32, (CHUNK, win_keys), 1)
    absdist = jnp.abs(WINDOW + qi - ko).astype(F32)

    def chunk_body(c, carry):
        r0 = pl.multiple_of(c * CHUNK, CHUNK)
        qc = q_buf[pl.ds(r0, CHUNK), :]
        kw = k_buf[pl.ds(r0, win_keys), :]
        vw = v_buf[pl.ds(r0, win_keys), :]
        if masked:
            valid = (t * tt + r0 - WINDOW + ko) >= 0
        for kh in range(N_KV):
            kh_cols = slice(kh * HEAD_DIM, (kh + 1) * HEAD_DIM)
            kwh = kw[:, kh_cols]
            vwh = vw[:, kh_cols]
            for g in range(GROUP):
                h = kh * GROUP + g
                h_cols = slice(h * HEAD_DIM, (h + 1) * HEAD_DIM)
                slope = 2.0 ** (-8.0 * (h + 1) / N_HEADS)
                s = lax.dot_general(qc[:, h_cols], kwh, (((1,), (1,)), ((), ())), preferred_element_type=F32)
                s = s - slope * absdist
                if masked:
                    s = jnp.where(valid, s, NEG_INF)
                sink = sink_ref[h]
                m = jnp.maximum(jnp.max(s, axis=-1, keepdims=True), sink)
                p = jnp.exp(s - m)
                denom = jnp.sum(p, axis=-1, keepdims=True) + jnp.exp(sink - m)
                o = _dot(p.astype(BF16), vwh) / denom
                o_buf[pl.ds(r0, CHUNK), h_cols] = o.astype(BF16)
        return carry

    lax.fori_loop(0, tt // CHUNK, chunk_body, 0)

    out = _dot(o_buf[...], w_out_ref[...])
    y_ref[...] = _layer_norm(alpha * x + out, lng_ref[...], lnb_ref[...])
    k_buf[0:WINDOW, :] = k_buf[tt:tt + WINDOW, :]
    v_buf[0:WINDOW, :] = v_buf[tt:tt + WINDOW, :]


def _mixer_c_call(x, k0, v0, wqkv, sinks, w_out, ln_g, ln_b, alpha, tt, masked):
    bsz, seq, d_model = x.shape
    d_q = N_HEADS * HEAD_DIM
    d_kv = N_KV * HEAD_DIM
    assert seq % tt == 0 and tt % CHUNK == 0
    assert tt >= WINDOW or seq == tt
    grid = (bsz, seq // tt)
    consts = (wqkv, w_out, ln_g, ln_b)
    in_specs = [
        pl.BlockSpec(memory_space=pltpu.SMEM),
        pl.BlockSpec((None, tt, d_model), lambda b, t: (b, t, 0)),
        pl.BlockSpec((None, WINDOW, d_kv), lambda b, t: (b, 0, 0)),
        pl.BlockSpec((None, WINDOW, d_kv), lambda b, t: (b, 0, 0)),
    ] + [_const_spec(c) for c in consts]
    out_shape = (
        jax.ShapeDtypeStruct((bsz, seq, d_model), F32),
        jax.ShapeDtypeStruct((bsz, WINDOW, d_kv), F32),
        jax.ShapeDtypeStruct((bsz, WINDOW, d_kv), F32),
    )
    out_specs = (
        pl.BlockSpec((None, tt, d_model), lambda b, t: (b, t, 0)),
        pl.BlockSpec((None, WINDOW, d_kv), lambda b, t: (b, 0, 0)),
        pl.BlockSpec((None, WINDOW, d_kv), lambda b, t: (b, 0, 0)),
    )
    scratch = [
        pltpu.VMEM((tt, d_q), BF16),
        pltpu.VMEM((tt + WINDOW, d_kv), BF16),
        pltpu.VMEM((tt + WINDOW, d_kv), BF16),
        pltpu.VMEM((tt, d_q), BF16),
    ]
    return pl.pallas_call(
        functools.partial(_mixer_c_body, alpha, masked),
        grid=grid, in_specs=in_specs, out_specs=out_specs, out_shape=out_shape,
        scratch_shapes=scratch,
        compiler_params=pltpu.CompilerParams(
            dimension_semantics=("arbitrary", "arbitrary"), vmem_limit_bytes=VMEM_LIMIT_BYTES),
        name="mixer_c",
    )(sinks, x, k0, v0, *consts)


def _ffn_body(alpha, x_ref, wg_ref, wu_ref, wd_ref, lng_ref, lnb_ref, y_ref, xb_buf, acc_buf):
    n_chunks = wg_ref.shape[0]
    xb_buf[...] = x_ref[...].astype(BF16)
    acc_buf[...] = jnp.zeros_like(acc_buf)

    def chunk_body(c, carry):
        xb = xb_buf[...]
        g = _dot(xb, wg_ref[c])
        u = _dot(xb, wu_ref[c])
        hid = (g * _sigmoid(g) * u).astype(BF16)
        acc_buf[...] += _dot(hid, wd_ref[c])
        return carry

    lax.fori_loop(0, n_chunks, chunk_body, 0)
    y_ref[...] = _layer_norm(alpha * x_ref[...] + acc_buf[...], lng_ref[...], lnb_ref[...])


def _ffn_call(x2d, wg, wu, wd, ln_g, ln_b, alpha, tm):
    rows, d_model = x2d.shape
    assert rows % tm == 0
    consts = (wg, wu, wd, ln_g, ln_b)

    def const_spec(arr):
        zeros = (0,) * arr.ndim
        return pl.BlockSpec(arr.shape, lambda i: zeros)

    return pl.pallas_call(
        functools.partial(_ffn_body, alpha),
        grid=(rows // tm,),
        in_specs=[pl.BlockSpec((tm, d_model), lambda i: (i, 0))] + [const_spec(c) for c in consts],
        out_specs=pl.BlockSpec((tm, d_model), lambda i: (i, 0)),
        out_shape=jax.ShapeDtypeStruct((rows, d_model), F32),
        scratch_shapes=[pltpu.VMEM((tm, d_model), BF16), pltpu.VMEM((tm, d_model), F32)],
        compiler_params=pltpu.CompilerParams(
            dimension_semantics=("arbitrary",), vmem_limit_bytes=VMEM_LIMIT_BYTES),
        name="ffn",
    )(x2d, *consts)


def _ffn(x, wg, wu, wd, ln_g, ln_b, alpha, tm):
    bsz, seq, d_model = x.shape
    return _ffn_call(x.reshape(bsz * seq, d_model), wg, wu, wd, ln_g, ln_b, alpha, tm).reshape(x.shape)


def _row(v):
    return v.reshape(1, -1).astype(F32)


def _gate_tiles(ga_w, gx_w):
    heads, blk, _ = ga_w.shape
    per_tile = MXU_DIM // blk
    tiles = []
    for j in range(heads // per_tile):
        def diag(w):
            return jax.scipy.linalg.block_diag(*[w[j * per_tile + i] for i in range(per_tile)])
        tiles.append(jnp.concatenate([diag(ga_w), diag(gx_w)], axis=1))
    return jnp.stack(tiles).astype(BF16)


def _ffn_chunks(wg, wu, wd):
    d_model, d_ff = wg.shape
    n = d_ff // MXU_DIM
    wg_c = wg.reshape(d_model, n, MXU_DIM).transpose(1, 0, 2).astype(BF16)
    wu_c = wu.reshape(d_model, n, MXU_DIM).transpose(1, 0, 2).astype(BF16)
    wd_c = wd.reshape(n, MXU_DIM, d_model).astype(BF16)
    return wg_c, wu_c, wd_c


def kernel(x_prompt, x_sample, state_rec_h, state_rec_conv, state_cf_conv, cache_k, cache_v, w_in_ab, rec_conv_w, rec_conv_b, rec_gate_a_w, rec_gate_a_b, rec_gate_x_w, rec_gate_x_b, rec_lambda, cf_conv_w, cf_conv_b, cf_norm_g, cf_norm_b, w_out_ab, w_qkv, attn_sinks, w_out_c, ln_mix_g, ln_mix_b, w_ff_gate, w_ff_up, w_ff_down, ln_ff_g, ln_ff_b):
    depth = ln_mix_g.shape[0]
    alpha = (2 * depth) ** 0.25
    bp, seq_p, d_model = x_prompt.shape
    bs, seq_s, _ = x_sample.shape
    d_rec = rec_conv_w.shape[-1]
    d_conv = cf_conv_w.shape[-1]
    d_kv = N_KV * HEAD_DIM
    tt_p = min(seq_p, 256)
    tt_s = seq_s
    tm_p = min(bp * seq_p, 512)
    tm_s = bs * seq_s

    yp, ys = x_prompt, x_sample
    p_h, s_h, p_rc, s_rc, p_cf, s_cf, p_k, s_k, p_v, s_v = ([] for _ in range(10))
    for layer in range(depth):
        j = layer // 2
        ln_g, ln_b = _row(ln_mix_g[layer]), _row(ln_mix_b[layer])
        if layer % 2 == 0:
            prm = (w_in_ab[j].astype(BF16), rec_conv_w[j], _row(rec_conv_b[j]),
                   _gate_tiles(rec_gate_a_w[j], rec_gate_x_w[j]), _row(rec_gate_a_b[j]), _row(rec_gate_x_b[j]),
                   _row(rec_lambda[j]), cf_conv_w[j], _row(cf_conv_b[j]), _row(cf_norm_g[j]), _row(cf_norm_b[j]),
                   w_out_ab[j].astype(BF16))
            yp, rc, hl, cf = _mixer_ab_call(
                yp, jnp.zeros((bp, REC_CONV - 1, d_rec), F32), jnp.zeros((bp, d_rec), F32),
                jnp.zeros((bp, CONV_WIDTH - 1, d_conv), F32), prm, ln_g, ln_b, alpha, tt_p)
            p_rc.append(rc), p_h.append(hl), p_cf.append(cf)
            ys, rc, hl, cf = _mixer_ab_call(
                ys, state_rec_conv[j], state_rec_h[j], state_cf_conv[j], prm, ln_g, ln_b, alpha, tt_s)
            s_rc.append(rc), s_h.append(hl), s_cf.append(cf)
        else:
            wqkv, w_out = w_qkv[j].astype(BF16), w_out_c[j].astype(BF16)
            sinks = attn_sinks[j].astype(F32)
            zeros_kv = jnp.zeros((bp, WINDOW, d_kv), F32)
            yp, kk, vv = _mixer_c_call(yp, zeros_kv, zeros_kv, wqkv, sinks, w_out, ln_g, ln_b, alpha, tt_p, True)
            p_k.append(kk.reshape(bp, WINDOW, N_KV, HEAD_DIM)), p_v.append(vv.reshape(bp, WINDOW, N_KV, HEAD_DIM))
            ck = cache_k[j].reshape(bs, WINDOW, d_kv)
            cv = cache_v[j].reshape(bs, WINDOW, d_kv)
            ys, kk, vv = _mixer_c_call(ys, ck, cv, wqkv, sinks, w_out, ln_g, ln_b, alpha, tt_s, False)
            s_k.append(kk.reshape(bs, WINDOW, N_KV, HEAD_DIM)), s_v.append(vv.reshape(bs, WINDOW, N_KV, HEAD_DIM))
        wg, wu, wd = _ffn_chunks(w_ff_gate[layer], w_ff_up[layer], w_ff_down[layer])
        fg, fb = _row(ln_ff_g[layer]), _row(ln_ff_b[layer])
        yp = _ffn(yp, wg, wu, wd, fg, fb, alpha, tm_p)
        ys = _ffn(ys, wg, wu, wd, fg, fb, alpha, tm_s)
    return (yp, ys, jnp.stack(p_h), jnp.stack(s_h), jnp.stack(p_rc), jnp.stack(s_rc),
            jnp.stack(p_cf), jnp.stack(s_cf), jnp.stack(p_k), jnp.stack(s_k), jnp.stack(p_v), jnp.stack(s_v))
```

```python
import functools

import jax
import jax.numpy as jnp
from jax import lax
from jax.experimental import pallas as pl
from jax.experimental.pallas import tpu as pltpu

CHUNK = 64
WINDOW = 128
HEAD_DIM = 64
N_HEADS = 16
N_KV = 4
GROUP = N_HEADS // N_KV
REC_CONV = 4
CONV_WIDTH = 31
LRU_C = 8.0
LN_EPS = 1e-5
NEG_INF = -1e30

SUBLANES = 8
MXU_DIM = 256
CONV_ROWS = 32
VMEM_LIMIT_BYTES = 56 * 1024 * 1024

BF16 = jnp.bfloat16
F32 = jnp.float32


def _dot(a, b):
    return jnp.dot(a, b, preferred_element_type=F32)


def _layer_norm(z, g, b):
    mu = jnp.mean(z, axis=-1, keepdims=True)
    zc = z - mu
    var = jnp.mean(zc * zc, axis=-1, keepdims=True)
    return zc * lax.rsqrt(var + LN_EPS) * g + b


def _sigmoid(x):
    return jax.nn.sigmoid(x)


def _mixer_ab_body(alpha, x_ref, rc0_ref, h0_ref, cf0_ref, w_in_ref, rcw_ref, rcb_ref, wgate_ref,
                   gab_ref, gxb_ref, lam_ref, cfw_ref, cfb_ref, cfg_ref, cfbeta_ref, w_out_ref,
                   lng_ref, lnb_ref,
                   y_ref, rc_out_ref, h_out_ref, cf_out_ref,
                   xr_buf, g_buf, a_buf, b_buf, h_buf, hcar, shift_buf, c_buf):
    tt = x_ref.shape[0]
    d_rec = rcw_ref.shape[1]
    d_conv = cfw_ref.shape[1]
    rc_tail = REC_CONV - 1
    cf_tail = CONV_WIDTH - 1
    rc_base = SUBLANES - rc_tail
    cf_base = 32 - cf_tail
    t = pl.program_id(1)

    @pl.when(t == 0)
    def _():
        xr_buf[rc_base:SUBLANES, :] = rc0_ref[...]
        g_buf[cf_base:32, :] = cf0_ref[...]
        hcar[...] = jnp.broadcast_to(h0_ref[...], hcar.shape)

    x = x_ref[...]
    xb = x.astype(BF16)

    xr = _dot(xb, w_in_ref[:, 0:d_rec])
    xr_buf[SUBLANES:SUBLANES + tt, :] = xr
    xc = rcb_ref[...] + rcw_ref[rc_tail:rc_tail + 1, :] * xr
    for k in range(rc_tail):
        xc = xc + rcw_ref[k:k + 1, :] * xr_buf[rc_base + k:rc_base + k + tt, :]
    xcb = xc.astype(BF16)
    r_parts, i_parts = [], []
    for j in range(d_rec // MXU_DIM):
        lo, hi = j * MXU_DIM, (j + 1) * MXU_DIM
        gj = _dot(xcb[:, lo:hi], wgate_ref[j])
        r_parts.append(_sigmoid(gj[:, :MXU_DIM] + gab_ref[:, lo:hi]))
        i_parts.append(_sigmoid(gj[:, MXU_DIM:] + gxb_ref[:, lo:hi]))
    r = jnp.concatenate(r_parts, axis=1)
    i = jnp.concatenate(i_parts, axis=1)
    nlam = -lam_ref[...]
    softplus = jnp.maximum(nlam, 0.0) + jnp.log1p(jnp.exp(-jnp.abs(nlam)))
    log_a = (-LRU_C * softplus) * r
    a = jnp.exp(log_a)
    bu = jnp.sqrt(1.0 - a * a) * (i * xc)

    groups = tt // SUBLANES
    a3 = a.reshape(groups, SUBLANES, d_rec)
    b3 = bu.reshape(groups, SUBLANES, d_rec)
    sub = lax.broadcasted_iota(jnp.int32, a3.shape, 1)
    d = 1
    while d < SUBLANES:
        keep = sub >= d
        a_prev = jnp.where(keep, pltpu.roll(a3, d, axis=1), 1.0)
        b_prev = jnp.where(keep, pltpu.roll(b3, d, axis=1), 0.0)
        b3 = a3 * b_prev + b3
        a3 = a3 * a_prev
        d *= 2
    a_buf[...] = a3.reshape(tt, d_rec)
    b_buf[...] = b3.reshape(tt, d_rec)

    h = hcar[...]
    for j in range(groups):
        rows = slice(j * SUBLANES, (j + 1) * SUBLANES)
        hj = a_buf[rows, :] * h + b_buf[rows, :]
        h_buf[rows, :] = hj
        h = jnp.broadcast_to(hj[SUBLANES - 1:SUBLANES, :], hj.shape)
    hcar[...] = h

    yr = _dot(xb, w_in_ref[:, d_rec:2 * d_rec])
    rec_out = h_buf[...] * jax.nn.gelu(yr)

    cv = _dot(xb, w_in_ref[:, 2 * d_rec:2 * d_rec + d_conv])
    cg = _dot(xb, w_in_ref[:, 2 * d_rec + d_conv:2 * d_rec + 2 * d_conv])
    gl = cv * _sigmoid(cg)
    g_buf[32:32 + tt, :] = gl
    n_ext = tt + 32
    g_ext = g_buf[...]
    for r in range(1, SUBLANES):
        shift_buf[r - 1] = pltpu.roll(g_ext, n_ext - r, axis=0)

    def conv_block(rb, carry):
        row0 = pl.multiple_of(rb * CONV_ROWS, CONV_ROWS)
        acc = jnp.broadcast_to(cfb_ref[...], (CONV_ROWS, d_conv))
        for k in range(CONV_WIDTH):
            q, r = divmod(cf_base + k, SUBLANES)
            if r == 0:
                src = g_buf[pl.ds(row0 + q * SUBLANES, CONV_ROWS), :]
            else:
                src = shift_buf[r - 1, pl.ds(row0 + q * SUBLANES, CONV_ROWS), :]
            acc = acc + cfw_ref[k:k + 1, :] * src
        c_buf[pl.ds(row0, CONV_ROWS), :] = acc
        return carry

    lax.fori_loop(0, tt // CONV_ROWS, conv_block, 0)
    cn = _layer_norm(c_buf[...], cfg_ref[...], cfbeta_ref[...])
    c2 = cn * _sigmoid(cn)

    out = _dot(rec_out.astype(BF16), w_out_ref[0:d_rec, :]) + _dot(c2.astype(BF16), w_out_ref[d_rec:d_rec + d_conv, :])
    y_ref[...] = _layer_norm(alpha * x + out, lng_ref[...], lnb_ref[...])

    new_rc = xr_buf[SUBLANES + tt - rc_tail:SUBLANES + tt, :]
    new_cf = g_buf[32 + tt - cf_tail:32 + tt, :]
    xr_buf[rc_base:SUBLANES, :] = new_rc
    g_buf[cf_base:32, :] = new_cf
    rc_out_ref[...] = new_rc
    cf_out_ref[...] = new_cf
    h_out_ref[...] = h[0:1, :]


def _const_spec(arr):
    zeros = (0,) * arr.ndim
    return pl.BlockSpec(arr.shape, lambda b, t: zeros)


def _mixer_ab_call(x, rc0, h0, cf0, prm, ln_g, ln_b, alpha, tt):
    bsz, seq, d_model = x.shape
    (w_in, rcw, rcb, wgate, gab, gxb, lam, cfw, cfb, cfg, cfbeta, w_out) = prm
    d_rec, d_conv = rcw.shape[1], cfw.shape[1]
    assert seq % tt == 0 and tt % CONV_ROWS == 0 and tt >= CONV_WIDTH - 1
    grid = (bsz, seq // tt)
    h0 = h0.reshape(bsz, 1, d_rec)
    consts = (w_in, rcw, rcb, wgate, gab, gxb, lam, cfw, cfb, cfg, cfbeta, w_out, ln_g, ln_b)
    in_specs = [
        pl.BlockSpec((None, tt, d_model), lambda b, t: (b, t, 0)),
        pl.BlockSpec((None, REC_CONV - 1, d_rec), lambda b, t: (b, 0, 0)),
        pl.BlockSpec((None, 1, d_rec), lambda b, t: (b, 0, 0)),
        pl.BlockSpec((None, CONV_WIDTH - 1, d_conv), lambda b, t: (b, 0, 0)),
    ] + [_const_spec(c) for c in consts]
    out_shape = (
        jax.ShapeDtypeStruct((bsz, seq, d_model), F32),
        jax.ShapeDtypeStruct((bsz, REC_CONV - 1, d_rec), F32),
        jax.ShapeDtypeStruct((bsz, 1, d_rec), F32),
        jax.ShapeDtypeStruct((bsz, CONV_WIDTH - 1, d_conv), F32),
    )
    out_specs = (
        pl.BlockSpec((None, tt, d_model), lambda b, t: (b, t, 0)),
        pl.BlockSpec((None, REC_CONV - 1, d_rec), lambda b, t: (b, 0, 0)),
        pl.BlockSpec((None, 1, d_rec), lambda b, t: (b, 0, 0)),
        pl.BlockSpec((None, CONV_WIDTH - 1, d_conv), lambda b, t: (b, 0, 0)),
    )
    scratch = [
        pltpu.VMEM((tt + SUBLANES, d_rec), F32),
        pltpu.VMEM((tt + 32, d_conv), F32),
        pltpu.VMEM((tt, d_rec), F32),
        pltpu.VMEM((tt, d_rec), F32),
        pltpu.VMEM((tt, d_rec), F32),
        pltpu.VMEM((SUBLANES, d_rec), F32),
        pltpu.VMEM((SUBLANES - 1, tt + 32, d_conv), F32),
        pltpu.VMEM((tt, d_conv), F32),
    ]
    y, rc, h, cf = pl.pallas_call(
        functools.partial(_mixer_ab_body, alpha),
        grid=grid, in_specs=in_specs, out_specs=out_specs, out_shape=out_shape,
        scratch_shapes=scratch,
        compiler_params=pltpu.CompilerParams(
            dimension_semantics=("arbitrary", "arbitrary"), vmem_limit_bytes=VMEM_LIMIT_BYTES),
        name="mixer_ab",
    )(x, rc0, h0, cf0, *consts)
    return y, rc, h.reshape(bsz, d_rec), cf


def _mixer_c_body(alpha, masked, sink_ref, x_ref, k0_ref, v0_ref, wqkv_ref, w_out_ref, lng_ref, lnb_ref,
                  y_ref, k_out_ref, v_out_ref,
                  q_buf, k_buf, v_buf, o_buf, bias_buf, sink_buf):
    tt = x_ref.shape[0]
    d_q = N_HEADS * HEAD_DIM
    d_kv = N_KV * HEAD_DIM
    win_keys = WINDOW + CHUNK
    stack = GROUP * CHUNK
    t = pl.program_id(1)

    @pl.when(t == 0)
    def _():
        k_buf[0:WINDOW, :] = k0_ref[...].astype(BF16)
        v_buf[0:WINDOW, :] = v0_ref[...].astype(BF16)
        row = lax.broadcasted_iota(jnp.int32, (stack, win_keys), 0)
        ko = lax.broadcasted_iota(jnp.int32, (stack, win_keys), 1)
        absdist = jnp.abs(WINDOW + (row % CHUNK) - ko).astype(F32)
        row1 = lax.broadcasted_iota(jnp.int32, (stack, 1), 0)
        for kh in range(N_KV):
            slope = jnp.zeros((stack, win_keys), F32)
            sink = jnp.zeros((stack, 1), F32)
            for g in range(GROUP):
                h = kh * GROUP + g
                slope = jnp.where(row // CHUNK == g, 2.0 ** (-8.0 * (h + 1) / N_HEADS), slope)
                sink = jnp.where(row1 // CHUNK == g, sink_ref[h], sink)
            bias_buf[kh] = -slope * absdist
            sink_buf[kh] = sink

    x = x_ref[...]
    xb = x.astype(BF16)
    q_buf[...] = (_dot(xb, wqkv_ref[:, 0:d_q]) * (HEAD_DIM ** -0.5)).astype(BF16)
    k = _dot(xb, wqkv_ref[:, d_q:d_q + d_kv])
    v = _dot(xb, wqkv_ref[:, d_q + d_kv:d_q + 2 * d_kv])
    k_buf[WINDOW:WINDOW + tt, :] = k.astype(BF16)
    v_buf[WINDOW:WINDOW + tt, :] = v.astype(BF16)
    if tt >= WINDOW:
        k_out_ref[...] = k[tt - WINDOW:tt, :]
        v_out_ref[...] = v[tt - WINDOW:tt, :]
    else:
        k_out_ref[0:WINDOW - tt, :] = k0_ref[tt:WINDOW, :]
        v_out_ref[0:WINDOW - tt, :] = v0_ref[tt:WINDOW, :]
        k_out_ref[WINDOW - tt:WINDOW, :] = k
        v_out_ref[WINDOW - tt:WINDOW, :] = v

    def attend(r0, mask_keys):
        scores = []
        for kh in range(N_KV):
            q4 = q_buf[pl.ds(r0, CHUNK), kh * stack:(kh + 1) * stack]
            qs = jnp.concatenate([q4[:, g * HEAD_DIM:(g + 1) * HEAD_DIM] for g in range(GROUP)], axis=0)
            kwh = k_buf[pl.ds(r0, win_keys), kh * HEAD_DIM:(kh + 1) * HEAD_DIM]
            s = lax.dot_general(qs, kwh, (((1,), (1,)), ((), ())), preferred_element_type=F32)
            s = s + bias_buf[kh]
            if mask_keys:
                ko = lax.broadcasted_iota(jnp.int32, (stack, win_keys), 1)
                s = jnp.where((t * tt + r0 - WINDOW + ko) >= 0, s, NEG_INF)
            scores.append(s)
        probs, denoms = [], []
        for kh in range(N_KV):
            s = scores[kh]
            sink = sink_buf[kh]
            m = jnp.maximum(jnp.max(s, axis=-1, keepdims=True), sink)
            p = jnp.exp(s - m)
            denoms.append(jnp.sum(p, axis=-1, keepdims=True) + jnp.exp(sink - m))
            probs.append(p.astype(BF16))
        for kh in range(N_KV):
            vwh = v_buf[pl.ds(r0, win_keys), kh * HEAD_DIM:(kh + 1) * HEAD_DIM]
            o = (_dot(probs[kh], vwh) / denoms[kh]).astype(BF16)
            for g in range(GROUP):
                h = kh * GROUP + g
                o_buf[pl.ds(r0, CHUNK), h * HEAD_DIM:(h + 1) * HEAD_DIM] = o[g * CHUNK:(g + 1) * CHUNK, :]

    def chunk_body(c, carry):
        r0 = pl.multiple_of(c * CHUNK, CHUNK)
        if masked:
            lax.cond(t * tt + r0 < WINDOW, lambda: attend(r0, True), lambda: attend(r0, False))
        else:
            attend(r0, False)
        return carry

    lax.fori_loop(0, tt // CHUNK, chunk_body, 0)

    out = _dot(o_buf[...], w_out_ref[...])
    y_ref[...] = _layer_norm(alpha * x + out, lng_ref[...], lnb_ref[...])
    k_buf[0:WINDOW, :] = k_buf[tt:tt + WINDOW, :]
    v_buf[0:WINDOW, :] = v_buf[tt:tt + WINDOW, :]


def _mixer_c_call(x, k0, v0, wqkv, sinks, w_out, ln_g, ln_b, alpha, tt, masked):
    bsz, seq, d_model = x.shape
    d_q = N_HEADS * HEAD_DIM
    d_kv = N_KV * HEAD_DIM
    assert seq % tt == 0 and tt % CHUNK == 0
    assert tt >= WINDOW or seq == tt
    grid = (bsz, seq // tt)
    consts = (wqkv, w_out, ln_g, ln_b)
    in_specs = [
        pl.BlockSpec(memory_space=pltpu.SMEM),
        pl.BlockSpec((None, tt, d_model), lambda b, t: (b, t, 0)),
        pl.BlockSpec((None, WINDOW, d_kv), lambda b, t: (b, 0, 0)),
        pl.BlockSpec((None, WINDOW, d_kv), lambda b, t: (b, 0, 0)),
    ] + [_const_spec(c) for c in consts]
    out_shape = (
        jax.ShapeDtypeStruct((bsz, seq, d_model), F32),
        jax.ShapeDtypeStruct((bsz, WINDOW, d_kv), F32),
        jax.ShapeDtypeStruct((bsz, WINDOW, d_kv), F32),
    )
    out_specs = (
        pl.BlockSpec((None, tt, d_model), lambda b, t: (b, t, 0)),
        pl.BlockSpec((None, WINDOW, d_kv), lambda b, t: (b, 0, 0)),
        pl.BlockSpec((None, WINDOW, d_kv), lambda b, t: (b, 0, 0)),
    )
    scratch = [
        pltpu.VMEM((tt, d_q), BF16),
        pltpu.VMEM((tt + WINDOW, d_kv), BF16),
        pltpu.VMEM((tt + WINDOW, d_kv), BF16),
        pltpu.VMEM((tt, d_q), BF16),
        pltpu.VMEM((N_KV, GROUP * CHUNK, WINDOW + CHUNK), F32),
        pltpu.VMEM((N_KV, GROUP * CHUNK, 1), F32),
    ]
    return pl.pallas_call(
        functools.partial(_mixer_c_body, alpha, masked),
        grid=grid, in_specs=in_specs, out_specs=out_specs, out_shape=out_shape,
        scratch_shapes=scratch,
        compiler_params=pltpu.CompilerParams(
            dimension_semantics=("arbitrary", "arbitrary"), vmem_limit_bytes=VMEM_LIMIT_BYTES),
        name="mixer_c",
    )(sinks, x, k0, v0, *consts)


def _ffn_body(alpha, x_ref, wg_ref, wu_ref, wd_ref, lng_ref, lnb_ref, y_ref, xb_buf, acc_buf):
    n_chunks = wg_ref.shape[0]
    xb_buf[...] = x_ref[...].astype(BF16)

    def down(c):
        xb = xb_buf[...]
        g = _dot(xb, wg_ref[c])
        u = _dot(xb, wu_ref[c])
        hid = (g * _sigmoid(g) * u).astype(BF16)
        return _dot(hid, wd_ref[c])

    acc_buf[...] = down(0)

    def chunk_body(c, carry):
        acc_buf[...] += down(c)
        return carry

    lax.fori_loop(1, n_chunks, chunk_body, 0)
    y_ref[...] = _layer_norm(alpha * x_ref[...] + acc_buf[...], lng_ref[...], lnb_ref[...])


def _ffn_call(x2d, wg, wu, wd, ln_g, ln_b, alpha, tm):
    rows, d_model = x2d.shape
    assert rows % tm == 0
    consts = (wg, wu, wd, ln_g, ln_b)

    def const_spec(arr):
        zeros = (0,) * arr.ndim
        return pl.BlockSpec(arr.shape, lambda i: zeros, pipeline_mode=pl.Buffered(1))

    return pl.pallas_call(
        functools.partial(_ffn_body, alpha),
        grid=(rows // tm,),
        in_specs=[pl.BlockSpec((tm, d_model), lambda i: (i, 0))] + [const_spec(c) for c in consts],
        out_specs=pl.BlockSpec((tm, d_model), lambda i: (i, 0)),
        out_shape=jax.ShapeDtypeStruct((rows, d_model), F32),
        scratch_shapes=[pltpu.VMEM((tm, d_model), BF16), pltpu.VMEM((tm, d_model), F32)],
        compiler_params=pltpu.CompilerParams(
            dimension_semantics=("arbitrary",), vmem_limit_bytes=VMEM_LIMIT_BYTES),
        name="ffn",
    )(x2d, *consts)


def _ffn(x, wg, wu, wd, ln_g, ln_b, alpha, tm):
    bsz, seq, d_model = x.shape
    return _ffn_call(x.reshape(bsz * seq, d_model), wg, wu, wd, ln_g, ln_b, alpha, tm).reshape(x.shape)


def _row(v):
    return v.reshape(1, -1).astype(F32)


def _gate_tiles(ga_w, gx_w):
    heads, blk, _ = ga_w.shape
    per_tile = MXU_DIM // blk
    tiles = []
    for j in range(heads // per_tile):
        def diag(w):
            return jax.scipy.linalg.block_diag(*[w[j * per_tile + i] for i in range(per_tile)])
        tiles.append(jnp.concatenate([diag(ga_w), diag(gx_w)], axis=1))
    return jnp.stack(tiles).astype(BF16)


def _ffn_chunks(wg, wu, wd):
    d_model, d_ff = wg.shape
    n = d_ff // MXU_DIM
    wg_c = wg.reshape(d_model, n, MXU_DIM).transpose(1, 0, 2).astype(BF16)
    wu_c = wu.reshape(d_model, n, MXU_DIM).transpose(1, 0, 2).astype(BF16)
    wd_c = wd.reshape(n, MXU_DIM, d_model).astype(BF16)
    return wg_c, wu_c, wd_c


def kernel(x_prompt, x_sample, state_rec_h, state_rec_conv, state_cf_conv, cache_k, cache_v, w_in_ab, rec_conv_w, rec_conv_b, rec_gate_a_w, rec_gate_a_b, rec_gate_x_w, rec_gate_x_b, rec_lambda, cf_conv_w, cf_conv_b, cf_norm_g, cf_norm_b, w_out_ab, w_qkv, attn_sinks, w_out_c, ln_mix_g, ln_mix_b, w_ff_gate, w_ff_up, w_ff_down, ln_ff_g, ln_ff_b):
    depth = ln_mix_g.shape[0]
    alpha = (2 * depth) ** 0.25
    bp, seq_p, d_model = x_prompt.shape
    bs, seq_s, _ = x_sample.shape
    d_rec = rec_conv_w.shape[-1]
    d_conv = cf_conv_w.shape[-1]
    d_kv = N_KV * HEAD_DIM
    tt_p = min(seq_p, 256)
    tt_s = seq_s
    tm_p = min(bp * seq_p, 1024)
    tm_s = bs * seq_s

    yp, ys = x_prompt, x_sample
    p_h, s_h, p_rc, s_rc, p_cf, s_cf, p_k, s_k, p_v, s_v = ([] for _ in range(10))
    for layer in range(depth):
        j = layer // 2
        ln_g, ln_b = _row(ln_mix_g[layer]), _row(ln_mix_b[layer])
        if layer % 2 == 0:
            prm = (w_in_ab[j].astype(BF16), rec_conv_w[j], _row(rec_conv_b[j]),
                   _gate_tiles(rec_gate_a_w[j], rec_gate_x_w[j]), _row(rec_gate_a_b[j]), _row(rec_gate_x_b[j]),
                   _row(rec_lambda[j]), cf_conv_w[j], _row(cf_conv_b[j]), _row(cf_norm_g[j]), _row(cf_norm_b[j]),
                   w_out_ab[j].astype(BF16))
            yp, rc, hl, cf = _mixer_ab_call(
                yp, jnp.zeros((bp, REC_CONV - 1, d_rec), F32), jnp.zeros((bp, d_rec), F32),
                jnp.zeros((bp, CONV_WIDTH - 1, d_conv), F32), prm, ln_g, ln_b, alpha, tt_p)
            p_rc.append(rc), p_h.append(hl), p_cf.append(cf)
            ys, rc, hl, cf = _mixer_ab_call(
                ys, state_rec_conv[j], state_rec_h[j], state_cf_conv[j], prm, ln_g, ln_b, alpha, tt_s)
            s_rc.append(rc), s_h.append(hl), s_cf.append(cf)
        else:
            wqkv, w_out = w_qkv[j].astype(BF16), w_out_c[j].astype(BF16)
            sinks = attn_sinks[j].astype(F32)
            zeros_kv = jnp.zeros((bp, WINDOW, d_kv), F32)
            yp, kk, vv = _mixer_c_call(yp, zeros_kv, zeros_kv, wqkv, sinks, w_out, ln_g, ln_b, alpha, tt_p, True)
            p_k.append(kk.reshape(bp, WINDOW, N_KV, HEAD_DIM)), p_v.append(vv.reshape(bp, WINDOW, N_KV, HEAD_DIM))
            ck = cache_k[j].reshape(bs, WINDOW, d_kv)
            cv = cache_v[j].reshape(bs, WINDOW, d_kv)
            ys, kk, vv = _mixer_c_call(ys, ck, cv, wqkv, sinks, w_out, ln_g, ln_b, alpha, tt_s, False)
            s_k.append(kk.reshape(bs, WINDOW, N_KV, HEAD_DIM)), s_v.append(vv.reshape(bs, WINDOW, N_KV, HEAD_DIM))
        wg, wu, wd = _ffn_chunks(w_ff_gate[layer], w_ff_up[layer], w_ff_down[layer])
        fg, fb = _row(ln_ff_g[layer]), _row(ln_ff_b[layer])
        yp = _ffn(yp, wg, wu, wd, fg, fb, alpha, tm_p)
        ys = _ffn(ys, wg, wu, wd, fg, fb, alpha, tm_s)
    return (yp, ys, jnp.stack(p_h), jnp.stack(s_h), jnp.stack(p_rc), jnp.stack(s_rc),
            jnp.stack(p_cf), jnp.stack(s_cf), jnp.stack(p_k), jnp.stack(s_k), jnp.stack(p_v), jnp.stack(s_v))
```

```python
import functools

import jax
import jax.numpy as jnp
from jax import lax
from jax.experimental import pallas as pl
from jax.experimental.pallas import tpu as pltpu

CHUNK = 64
WINDOW = 128
HEAD_DIM = 64
N_HEADS = 16
N_KV = 4
GROUP = N_HEADS // N_KV
REC_CONV = 4
CONV_WIDTH = 31
LRU_C = 8.0
LN_EPS = 1e-5
NEG_INF = -1e30

SUBLANES = 8
MXU_DIM = 256
CONV_ROWS = 32
VMEM_LIMIT_BYTES = 56 * 1024 * 1024

BF16 = jnp.bfloat16
F32 = jnp.float32


def _dot(a, b):
    return jnp.dot(a, b, preferred_element_type=F32)


def _layer_norm(z, g, b):
    mu = jnp.mean(z, axis=-1, keepdims=True)
    zc = z - mu
    var = jnp.mean(zc * zc, axis=-1, keepdims=True)
    return zc * lax.rsqrt(var + LN_EPS) * g + b


def _sigmoid(x):
    return jax.nn.sigmoid(x)


def _mixer_ab_body(alpha, x_ref, rc0_ref, h0_ref, cf0_ref, w_in_ref, rcw_ref, rcb_ref, wgate_ref,
                   gab_ref, gxb_ref, lam_ref, cfw_ref, cfb_ref, cfg_ref, cfbeta_ref, w_out_ref,
                   lng_ref, lnb_ref,
                   y_ref, rc_out_ref, h_out_ref, cf_out_ref,
                   xr_buf, g_buf, a_buf, b_buf, h_buf, hcar, shift_buf, c_buf):
    tt = x_ref.shape[0]
    d_rec = rcw_ref.shape[1]
    d_conv = cfw_ref.shape[1]
    rc_tail = REC_CONV - 1
    cf_tail = CONV_WIDTH - 1
    rc_base = SUBLANES - rc_tail
    cf_base = 32 - cf_tail
    t = pl.program_id(1)

    @pl.when(t == 0)
    def _():
        xr_buf[rc_base:SUBLANES, :] = rc0_ref[...]
        g_buf[cf_base:32, :] = cf0_ref[...]
        hcar[...] = jnp.broadcast_to(h0_ref[...], hcar.shape)

    x = x_ref[...]
    xb = x.astype(BF16)

    xr = _dot(xb, w_in_ref[:, 0:d_rec])
    xr_buf[SUBLANES:SUBLANES + tt, :] = xr
    xc = rcb_ref[...] + rcw_ref[rc_tail:rc_tail + 1, :] * xr
    for k in range(rc_tail):
        xc = xc + rcw_ref[k:k + 1, :] * xr_buf[rc_base + k:rc_base + k + tt, :]
    xcb = xc.astype(BF16)
    r_parts, i_parts = [], []
    for j in range(d_rec // MXU_DIM):
        lo, hi = j * MXU_DIM, (j + 1) * MXU_DIM
        gj = _dot(xcb[:, lo:hi], wgate_ref[j])
        r_parts.append(_sigmoid(gj[:, :MXU_DIM] + gab_ref[:, lo:hi]))
        i_parts.append(_sigmoid(gj[:, MXU_DIM:] + gxb_ref[:, lo:hi]))
    r = jnp.concatenate(r_parts, axis=1)
    i = jnp.concatenate(i_parts, axis=1)
    nlam = -lam_ref[...]
    softplus = jnp.maximum(nlam, 0.0) + jnp.log1p(jnp.exp(-jnp.abs(nlam)))
    log_a = (-LRU_C * softplus) * r
    a = jnp.exp(log_a)
    bu = jnp.sqrt(1.0 - a * a) * (i * xc)

    groups = tt // SUBLANES
    a3 = a.reshape(groups, SUBLANES, d_rec)
    b3 = bu.reshape(groups, SUBLANES, d_rec)
    sub = lax.broadcasted_iota(jnp.int32, a3.shape, 1)
    d = 1
    while d < SUBLANES:
        keep = sub >= d
        a_prev = jnp.where(keep, pltpu.roll(a3, d, axis=1), 1.0)
        b_prev = jnp.where(keep, pltpu.roll(b3, d, axis=1), 0.0)
        b3 = a3 * b_prev + b3
        a3 = a3 * a_prev
        d *= 2
    a_buf[...] = a3.reshape(tt, d_rec)
    b_buf[...] = b3.reshape(tt, d_rec)

    h = hcar[...]
    for j in range(groups):
        rows = slice(j * SUBLANES, (j + 1) * SUBLANES)
        hj = a_buf[rows, :] * h + b_buf[rows, :]
        h_buf[rows, :] = hj
        h = jnp.broadcast_to(hj[SUBLANES - 1:SUBLANES, :], hj.shape)
    hcar[...] = h

    yr = _dot(xb, w_in_ref[:, d_rec:2 * d_rec])
    rec_out = h_buf[...] * jax.nn.gelu(yr)

    cv = _dot(xb, w_in_ref[:, 2 * d_rec:2 * d_rec + d_conv])
    cg = _dot(xb, w_in_ref[:, 2 * d_rec + d_conv:2 * d_rec + 2 * d_conv])
    gl = cv * _sigmoid(cg)
    g_buf[32:32 + tt, :] = gl
    n_ext = tt + 32
    g_ext = g_buf[...]
    for r in range(1, SUBLANES):
        shift_buf[r - 1] = pltpu.roll(g_ext, n_ext - r, axis=0)

    def conv_block(rb, carry):
        row0 = pl.multiple_of(rb * CONV_ROWS, CONV_ROWS)
        acc = jnp.broadcast_to(cfb_ref[...], (CONV_ROWS, d_conv))
        for k in range(CONV_WIDTH):
            q, r = divmod(cf_base + k, SUBLANES)
            if r == 0:
                src = g_buf[pl.ds(row0 + q * SUBLANES, CONV_ROWS), :]
            else:
                src = shift_buf[r - 1, pl.ds(row0 + q * SUBLANES, CONV_ROWS), :]
            acc = acc + cfw_ref[k:k + 1, :] * src
        c_buf[pl.ds(row0, CONV_ROWS), :] = acc
        return carry

    lax.fori_loop(0, tt // CONV_ROWS, conv_block, 0)
    cn = _layer_norm(c_buf[...], cfg_ref[...], cfbeta_ref[...])
    c2 = cn * _sigmoid(cn)

    out = _dot(rec_out.astype(BF16), w_out_ref[0:d_rec, :]) + _dot(c2.astype(BF16), w_out_ref[d_rec:d_rec + d_conv, :])
    y_ref[...] = _layer_norm(alpha * x + out, lng_ref[...], lnb_ref[...])

    new_rc = xr_buf[SUBLANES + tt - rc_tail:SUBLANES + tt, :]
    new_cf = g_buf[32 + tt - cf_tail:32 + tt, :]
    xr_buf[rc_base:SUBLANES, :] = new_rc
    g_buf[cf_base:32, :] = new_cf
    rc_out_ref[...] = new_rc
    cf_out_ref[...] = new_cf
    h_out_ref[...] = h[0:1, :]


def _const_spec(arr):
    zeros = (0,) * arr.ndim
    return pl.BlockSpec(arr.shape, lambda b, t: zeros)


def _mixer_ab_call(x, rc0, h0, cf0, prm, ln_g, ln_b, alpha, tt):
    bsz, seq, d_model = x.shape
    (w_in, rcw, rcb, wgate, gab, gxb, lam, cfw, cfb, cfg, cfbeta, w_out) = prm
    d_rec, d_conv = rcw.shape[1], cfw.shape[1]
    assert seq % tt == 0 and tt % CONV_ROWS == 0 and tt >= CONV_WIDTH - 1
    grid = (bsz, seq // tt)
    h0 = h0.reshape(bsz, 1, d_rec)
    consts = (w_in, rcw, rcb, wgate, gab, gxb, lam, cfw, cfb, cfg, cfbeta, w_out, ln_g, ln_b)
    in_specs = [
        pl.BlockSpec((None, tt, d_model), lambda b, t: (b, t, 0)),
        pl.BlockSpec((None, REC_CONV - 1, d_rec), lambda b, t: (b, 0, 0)),
        pl.BlockSpec((None, 1, d_rec), lambda b, t: (b, 0, 0)),
        pl.BlockSpec((None, CONV_WIDTH - 1, d_conv), lambda b, t: (b, 0, 0)),
    ] + [_const_spec(c) for c in consts]
    out_shape = (
        jax.ShapeDtypeStruct((bsz, seq, d_model), F32),
        jax.ShapeDtypeStruct((bsz, REC_CONV - 1, d_rec), F32),
        jax.ShapeDtypeStruct((bsz, 1, d_rec), F32),
        jax.ShapeDtypeStruct((bsz, CONV_WIDTH - 1, d_conv), F32),
    )
    out_specs = (
        pl.BlockSpec((None, tt, d_model), lambda b, t: (b, t, 0)),
        pl.BlockSpec((None, REC_CONV - 1, d_rec), lambda b, t: (b, 0, 0)),
        pl.BlockSpec((None, 1, d_rec), lambda b, t: (b, 0, 0)),
        pl.BlockSpec((None, CONV_WIDTH - 1, d_conv), lambda b, t: (b, 0, 0)),
    )
    scratch = [
        pltpu.VMEM((tt + SUBLANES, d_rec), F32),
        pltpu.VMEM((tt + 32, d_conv), F32),
        pltpu.VMEM((tt, d_rec), F32),
        pltpu.VMEM((tt, d_rec), F32),
        pltpu.VMEM((tt, d_rec), F32),
        pltpu.VMEM((SUBLANES, d_rec), F32),
        pltpu.VMEM((SUBLANES - 1, tt + 32, d_conv), F32),
        pltpu.VMEM((tt, d_conv), F32),
    ]
    y, rc, h, cf = pl.pallas_call(
        functools.partial(_mixer_ab_body, alpha),
        grid=grid, in_specs=in_specs, out_specs=out_specs, out_shape=out_shape,
        scratch_shapes=scratch,
        compiler_params=pltpu.CompilerParams(
            dimension_semantics=("arbitrary", "arbitrary"), vmem_limit_bytes=VMEM_LIMIT_BYTES),
        name="mixer_ab",
    )(x, rc0, h0, cf0, *consts)
    return y, rc, h.reshape(bsz, d_rec), cf


def _mixer_c_body(alpha, masked, sink_ref, x_ref, k0_ref, v0_ref, wqkv_ref, w_out_ref, lng_ref, lnb_ref,
                  y_ref, k_out_ref, v_out_ref,
                  q_buf, k_buf, v_buf, o_buf, bias_buf, sink_buf):
    tt = x_ref.shape[0]
    d_q = N_HEADS * HEAD_DIM
    d_kv = N_KV * HEAD_DIM
    win_keys = WINDOW + CHUNK
    stack = GROUP * CHUNK
    t = pl.program_id(1)

    @pl.when(t == 0)
    def _():
        k_buf[0:WINDOW, :] = k0_ref[...].astype(BF16)
        v_buf[0:WINDOW, :] = v0_ref[...].astype(BF16)
        row = lax.broadcasted_iota(jnp.int32, (stack, win_keys), 0)
        ko = lax.broadcasted_iota(jnp.int32, (stack, win_keys), 1)
        absdist = jnp.abs(WINDOW + (row % CHUNK) - ko).astype(F32)
        row1 = lax.broadcasted_iota(jnp.int32, (stack, 1), 0)
        for kh in range(N_KV):
            slope = jnp.zeros((stack, win_keys), F32)
            sink = jnp.zeros((stack, 1), F32)
            for g in range(GROUP):
                h = kh * GROUP + g
                slope = jnp.where(row // CHUNK == g, 2.0 ** (-8.0 * (h + 1) / N_HEADS), slope)
                sink = jnp.where(row1 // CHUNK == g, sink_ref[h], sink)
            bias_buf[kh] = -slope * absdist
            sink_buf[kh] = sink

    x = x_ref[...]
    xb = x.astype(BF16)
    q_buf[...] = (_dot(xb, wqkv_ref[:, 0:d_q]) * (HEAD_DIM ** -0.5)).astype(BF16)
    k = _dot(xb, wqkv_ref[:, d_q:d_q + d_kv])
    v = _dot(xb, wqkv_ref[:, d_q + d_kv:d_q + 2 * d_kv])
    k_buf[WINDOW:WINDOW + tt, :] = k.astype(BF16)
    v_buf[WINDOW:WINDOW + tt, :] = v.astype(BF16)
    if tt >= WINDOW:
        k_out_ref[...] = k[tt - WINDOW:tt, :]
        v_out_ref[...] = v[tt - WINDOW:tt, :]
    else:
        k_out_ref[0:WINDOW - tt, :] = k0_ref[tt:WINDOW, :]
        v_out_ref[0:WINDOW - tt, :] = v0_ref[tt:WINDOW, :]
        k_out_ref[WINDOW - tt:WINDOW, :] = k
        v_out_ref[WINDOW - tt:WINDOW, :] = v

    def attend(r0, mask_keys):
        scores = []
        for kh in range(N_KV):
            q4 = q_buf[pl.ds(r0, CHUNK), kh * stack:(kh + 1) * stack]
            qs = jnp.concatenate([q4[:, g * HEAD_DIM:(g + 1) * HEAD_DIM] for g in range(GROUP)], axis=0)
            kwh = k_buf[pl.ds(r0, win_keys), kh * HEAD_DIM:(kh + 1) * HEAD_DIM]
            s = lax.dot_general(qs, kwh, (((1,), (1,)), ((), ())), preferred_element_type=F32)
            s = s + bias_buf[kh]
            if mask_keys:
                ko = lax.broadcasted_iota(jnp.int32, (stack, win_keys), 1)
                s = jnp.where((t * tt + r0 - WINDOW + ko) >= 0, s, NEG_INF)
            scores.append(s)
        probs, denoms = [], []
        for kh in range(N_KV):
            s = scores[kh]
            sink = sink_buf[kh]
            m = jnp.maximum(jnp.max(s, axis=-1, keepdims=True), sink)
            p = jnp.exp(s - m)
            denoms.append(jnp.sum(p, axis=-1, keepdims=True) + jnp.exp(sink - m))
            probs.append(p.astype(BF16))
        for kh in range(N_KV):
            vwh = v_buf[pl.ds(r0, win_keys), kh * HEAD_DIM:(kh + 1) * HEAD_DIM]
            o = (_dot(probs[kh], vwh) / denoms[kh]).astype(BF16)
            for g in range(GROUP):
                h = kh * GROUP + g
                o_buf[pl.ds(r0, CHUNK), h * HEAD_DIM:(h + 1) * HEAD_DIM] = o[g * CHUNK:(g + 1) * CHUNK, :]

    def chunk_body(c, carry):
        r0 = pl.multiple_of(c * CHUNK, CHUNK)
        if masked:
            lax.cond(t * tt + r0 < WINDOW, lambda: attend(r0, True), lambda: attend(r0, False))
        else:
            attend(r0, False)
        return carry

    lax.fori_loop(0, tt // CHUNK, chunk_body, 0)

    out = _dot(o_buf[...], w_out_ref[...])
    y_ref[...] = _layer_norm(alpha * x + out, lng_ref[...], lnb_ref[...])
    k_buf[0:WINDOW, :] = k_buf[tt:tt + WINDOW, :]
    v_buf[0:WINDOW, :] = v_buf[tt:tt + WINDOW, :]


def _mixer_c_call(x, k0, v0, wqkv, sinks, w_out, ln_g, ln_b, alpha, tt, masked):
    bsz, seq, d_model = x.shape
    d_q = N_HEADS * HEAD_DIM
    d_kv = N_KV * HEAD_DIM
    assert seq % tt == 0 and tt % CHUNK == 0
    assert tt >= WINDOW or seq == tt
    grid = (bsz, seq // tt)
    consts = (wqkv, w_out, ln_g, ln_b)
    in_specs = [
        pl.BlockSpec(memory_space=pltpu.SMEM),
        pl.BlockSpec((None, tt, d_model), lambda b, t: (b, t, 0)),
        pl.BlockSpec((None, WINDOW, d_kv), lambda b, t: (b, 0, 0)),
        pl.BlockSpec((None, WINDOW, d_kv), lambda b, t: (b, 0, 0)),
    ] + [_const_spec(c) for c in consts]
    out_shape = (
        jax.ShapeDtypeStruct((bsz, seq, d_model), F32),
        jax.ShapeDtypeStruct((bsz, WINDOW, d_kv), F32),
        jax.ShapeDtypeStruct((bsz, WINDOW, d_kv), F32),
    )
    out_specs = (
        pl.BlockSpec((None, tt, d_model), lambda b, t: (b, t, 0)),
        pl.BlockSpec((None, WINDOW, d_kv), lambda b, t: (b, 0, 0)),
        pl.BlockSpec((None, WINDOW, d_kv), lambda b, t: (b, 0, 0)),
    )
    scratch = [
        pltpu.VMEM((tt, d_q), BF16),
        pltpu.VMEM((tt + WINDOW, d_kv), BF16),
        pltpu.VMEM((tt + WINDOW, d_kv), BF16),
        pltpu.VMEM((tt, d_q), BF16),
        pltpu.VMEM((N_KV, GROUP * CHUNK, WINDOW + CHUNK), F32),
        pltpu.VMEM((N_KV, GROUP * CHUNK, 1), F32),
    ]
    return pl.pallas_call(
        functools.partial(_mixer_c_body, alpha, masked),
        grid=grid, in_specs=in_specs, out_specs=out_specs, out_shape=out_shape,
        scratch_shapes=scratch,
        compiler_params=pltpu.CompilerParams(
            dimension_semantics=("arbitrary", "arbitrary"), vmem_limit_bytes=VMEM_LIMIT_BYTES),
        name="mixer_c",
    )(sinks, x, k0, v0, *consts)


QBLK = 2 * CHUNK
KBLK = WINDOW + QBLK
ONES_ROWS = 16
ATTN_SKEW = 3


def _mixer_c_prompt_body(alpha, sink_ref, x_ref, wqT_ref, wk_ref, wv_ref, wvT_ref, w_out_ref, lng_ref, lnb_ref,
                         y_ref, k_out_ref, v_out_ref,
                         qT_buf, k_buf, vT_buf, oT_buf, bias_buf, sink_buf):
    tt = x_ref.shape[0]
    t = pl.program_id(1)
    n_pairs = N_HEADS // 2
    lanes = 2 * QBLK

    @pl.when(t == 0)
    def _():
        vT_buf[...] = jnp.ones(vT_buf.shape, BF16)
        k_buf[:, 0:WINDOW, :] = jnp.zeros((N_KV, WINDOW, HEAD_DIM), BF16)
        kr = lax.broadcasted_iota(jnp.int32, (KBLK, lanes), 0)
        ln = lax.broadcasted_iota(jnp.int32, (KBLK, lanes), 1)
        ql = ln % QBLK
        absdist = jnp.abs(ql + WINDOW - kr).astype(F32)
        kc, qc = kr // CHUNK, ql // CHUNK
        in_window = (kc >= qc) & (kc <= qc + WINDOW // CHUNK)
        ln1 = lax.broadcasted_iota(jnp.int32, (SUBLANES, lanes), 1)
        for pr in range(n_pairs):
            s0 = 2.0 ** (-8.0 * (2 * pr + 1) / N_HEADS)
            s1 = 2.0 ** (-8.0 * (2 * pr + 2) / N_HEADS)
            slope = jnp.where(ln < QBLK, s0, s1)
            bias_buf[pr] = jnp.where(in_window, -slope * absdist, NEG_INF)
            sink_buf[pr] = jnp.where(ln1 < QBLK, sink_ref[2 * pr], sink_ref[2 * pr + 1])

    x = x_ref[...]
    xb = x.astype(BF16)
    nt_dims = (((1,), (1,)), ((), ()))
    qT = lax.dot_general(wqT_ref[...], xb, nt_dims, preferred_element_type=F32)
    qT_buf[...] = (qT * (HEAD_DIM ** -0.5)).astype(BF16)
    k = _dot(xb, wk_ref[...])
    kb = k.astype(BF16)
    vT = lax.dot_general(wvT_ref[...], xb, nt_dims, preferred_element_type=F32)
    for kh in range(N_KV):
        k_buf[kh, WINDOW:WINDOW + tt, :] = kb[:, kh * HEAD_DIM:(kh + 1) * HEAD_DIM]
        vT_buf[kh, 0:HEAD_DIM, WINDOW:WINDOW + tt] = vT[kh * HEAD_DIM:(kh + 1) * HEAD_DIM, :].astype(BF16)
    k_out_ref[...] = k[tt - WINDOW:tt, :]

    @pl.when(t == pl.num_programs(1) - 1)
    def _():
        v_out_ref[...] = _dot(xb[tt - WINDOW:tt, :], wv_ref[...])

    def scores(qb, pr, key_lo):
        kh = pr // (GROUP // 2)
        q_lanes = slice(qb * QBLK, (qb + 1) * QBLK)
        h_rows = [slice((2 * pr + i) * HEAD_DIM, (2 * pr + i + 1) * HEAD_DIM) for i in range(2)]
        q2 = jnp.concatenate([qT_buf[h_rows[0], q_lanes], qT_buf[h_rows[1], q_lanes]], axis=1)
        keys = slice(qb * QBLK + key_lo, qb * QBLK + KBLK)
        return _dot(k_buf[kh, keys, :], q2) + bias_buf[pr, key_lo:KBLK, :]

    def finish(qb, pr, key_lo, sT):
        kh = pr // (GROUP // 2)
        q_lanes = slice(qb * QBLK, (qb + 1) * QBLK)
        h_rows = [slice((2 * pr + i) * HEAD_DIM, (2 * pr + i + 1) * HEAD_DIM) for i in range(2)]
        keys = slice(qb * QBLK + key_lo, qb * QBLK + KBLK)
        sink = sink_buf[pr][0:1, :]
        m = jnp.maximum(jnp.max(sT, axis=0, keepdims=True), sink)
        p = jnp.exp(sT - m)
        pv = _dot(vT_buf[kh, :, keys], p.astype(BF16))
        den = pv[HEAD_DIM:HEAD_DIM + 1, :] + jnp.exp(sink - m)
        oT = (pv[0:HEAD_DIM, :] * (1.0 / den)).astype(BF16)
        oT_buf[h_rows[0], q_lanes] = oT[:, 0:QBLK]
        oT_buf[h_rows[1], q_lanes] = oT[:, QBLK:lanes]

    def attend_all(first_key_lo):
        items = [(qb, pr, first_key_lo if qb == 0 else 0) for qb in range(tt // QBLK) for pr in range(n_pairs)]
        pending = []
        for item in items:
            pending.append((item, scores(*item)))
            if len(pending) > ATTN_SKEW:
                done, sT = pending.pop(0)
                finish(*done, sT)
        for done, sT in pending:
            finish(*done, sT)

    @pl.when(t == 0)
    def _():
        attend_all(WINDOW)

    @pl.when(t != 0)
    def _():
        attend_all(0)

    out = lax.dot_general(oT_buf[...], w_out_ref[...], (((0,), (0,)), ((), ())), preferred_element_type=F32)
    y_ref[...] = _layer_norm(alpha * x + out, lng_ref[...], lnb_ref[...])
    k_buf[:, 0:WINDOW, :] = k_buf[:, tt:tt + WINDOW, :]
    vT_buf[:, 0:HEAD_DIM, 0:WINDOW] = vT_buf[:, 0:HEAD_DIM, tt:tt + WINDOW]


def _mixer_c_prompt_call(x, wqT, wk, wv, wvT, sinks, w_out, ln_g, ln_b, alpha, tt):
    bsz, seq, d_model = x.shape
    d_q = N_HEADS * HEAD_DIM
    d_kv = N_KV * HEAD_DIM
    assert seq % tt == 0 and tt % QBLK == 0 and tt >= WINDOW
    consts = (wqT, wk, wv, wvT, w_out, ln_g, ln_b)
    in_specs = [
        pl.BlockSpec(memory_space=pltpu.SMEM),
        pl.BlockSpec((None, tt, d_model), lambda b, t: (b, t, 0)),
    ] + [_const_spec(c) for c in consts]
    out_shape = (
        jax.ShapeDtypeStruct((bsz, seq, d_model), F32),
        jax.ShapeDtypeStruct((bsz, WINDOW, d_kv), F32),
        jax.ShapeDtypeStruct((bsz, WINDOW, d_kv), F32),
    )
    out_specs = (
        pl.BlockSpec((None, tt, d_model), lambda b, t: (b, t, 0)),
        pl.BlockSpec((None, WINDOW, d_kv), lambda b, t: (b, 0, 0)),
        pl.BlockSpec((None, WINDOW, d_kv), lambda b, t: (b, 0, 0)),
    )
    scratch = [
        pltpu.VMEM((d_q, tt), BF16),
        pltpu.VMEM((N_KV, tt + WINDOW, HEAD_DIM), BF16),
        pltpu.VMEM((N_KV, HEAD_DIM + ONES_ROWS, tt + WINDOW), BF16),
        pltpu.VMEM((d_q, tt), BF16),
        pltpu.VMEM((N_HEADS // 2, KBLK, 2 * QBLK), F32),
        pltpu.VMEM((N_HEADS // 2, SUBLANES, 2 * QBLK), F32),
    ]
    return pl.pallas_call(
        functools.partial(_mixer_c_prompt_body, alpha),
        grid=(bsz, seq // tt), in_specs=in_specs, out_specs=out_specs, out_shape=out_shape,
        scratch_shapes=scratch,
        compiler_params=pltpu.CompilerParams(
            dimension_semantics=("arbitrary", "arbitrary"), vmem_limit_bytes=VMEM_LIMIT_BYTES),
        name="mixer_c_prompt",
    )(sinks, x, *consts)


def _ffn_body(alpha, x_ref, wg_ref, wu_ref, wd_ref, lng_ref, lnb_ref, y_ref, xb_buf, acc_buf):
    n_chunks = wg_ref.shape[0]
    xb_buf[...] = x_ref[...].astype(BF16)

    def down(c):
        xb = xb_buf[...]
        g = _dot(xb, wg_ref[c])
        u = _dot(xb, wu_ref[c])
        hid = (g * _sigmoid(g) * u).astype(BF16)
        return _dot(hid, wd_ref[c])

    acc_buf[...] = down(0)

    def chunk_body(c, carry):
        acc_buf[...] += down(c)
        return carry

    lax.fori_loop(1, n_chunks, chunk_body, 0)
    y_ref[...] = _layer_norm(alpha * x_ref[...] + acc_buf[...], lng_ref[...], lnb_ref[...])


def _ffn_call(x2d, wg, wu, wd, ln_g, ln_b, alpha, tm):
    rows, d_model = x2d.shape
    assert rows % tm == 0
    consts = (wg, wu, wd, ln_g, ln_b)

    def const_spec(arr):
        zeros = (0,) * arr.ndim
        return pl.BlockSpec(arr.shape, lambda i: zeros, pipeline_mode=pl.Buffered(1))

    return pl.pallas_call(
        functools.partial(_ffn_body, alpha),
        grid=(rows // tm,),
        in_specs=[pl.BlockSpec((tm, d_model), lambda i: (i, 0))] + [const_spec(c) for c in consts],
        out_specs=pl.BlockSpec((tm, d_model), lambda i: (i, 0)),
        out_shape=jax.ShapeDtypeStruct((rows, d_model), F32),
        scratch_shapes=[pltpu.VMEM((tm, d_model), BF16), pltpu.VMEM((tm, d_model), F32)],
        compiler_params=pltpu.CompilerParams(
            dimension_semantics=("arbitrary",), vmem_limit_bytes=VMEM_LIMIT_BYTES),
        name="ffn",
    )(x2d, *consts)


def _ffn(x, wg, wu, wd, ln_g, ln_b, alpha, tm):
    bsz, seq, d_model = x.shape
    return _ffn_call(x.reshape(bsz * seq, d_model), wg, wu, wd, ln_g, ln_b, alpha, tm).reshape(x.shape)


def _row(v):
    return v.reshape(1, -1).astype(F32)


def _gate_tiles(ga_w, gx_w):
    heads, blk, _ = ga_w.shape
    per_tile = MXU_DIM // blk
    tiles = []
    for j in range(heads // per_tile):
        def diag(w):
            return jax.scipy.linalg.block_diag(*[w[j * per_tile + i] for i in range(per_tile)])
        tiles.append(jnp.concatenate([diag(ga_w), diag(gx_w)], axis=1))
    return jnp.stack(tiles).astype(BF16)


def _ffn_chunks(wg, wu, wd):
    d_model, d_ff = wg.shape
    n = d_ff // MXU_DIM
    wg_c = wg.reshape(d_model, n, MXU_DIM).transpose(1, 0, 2).astype(BF16)
    wu_c = wu.reshape(d_model, n, MXU_DIM).transpose(1, 0, 2).astype(BF16)
    wd_c = wd.reshape(n, MXU_DIM, d_model).astype(BF16)
    return wg_c, wu_c, wd_c


def kernel(x_prompt, x_sample, state_rec_h, state_rec_conv, state_cf_conv, cache_k, cache_v, w_in_ab, rec_conv_w, rec_conv_b, rec_gate_a_w, rec_gate_a_b, rec_gate_x_w, rec_gate_x_b, rec_lambda, cf_conv_w, cf_conv_b, cf_norm_g, cf_norm_b, w_out_ab, w_qkv, attn_sinks, w_out_c, ln_mix_g, ln_mix_b, w_ff_gate, w_ff_up, w_ff_down, ln_ff_g, ln_ff_b):
    depth = ln_mix_g.shape[0]
    alpha = (2 * depth) ** 0.25
    bp, seq_p, d_model = x_prompt.shape
    bs, seq_s, _ = x_sample.shape
    d_rec = rec_conv_w.shape[-1]
    d_conv = cf_conv_w.shape[-1]
    d_kv = N_KV * HEAD_DIM
    tt_ab = min(seq_p, 512)
    tt_c = min(seq_p, 512)
    tt_s = seq_s
    tm_p = min(bp * seq_p, 1024)
    tm_s = bs * seq_s

    yp, ys = x_prompt, x_sample
    p_h, s_h, p_rc, s_rc, p_cf, s_cf, p_k, s_k, p_v, s_v = ([] for _ in range(10))
    for layer in range(depth):
        j = layer // 2
        ln_g, ln_b = _row(ln_mix_g[layer]), _row(ln_mix_b[layer])
        if layer % 2 == 0:
            prm = (w_in_ab[j].astype(BF16), rec_conv_w[j], _row(rec_conv_b[j]),
                   _gate_tiles(rec_gate_a_w[j], rec_gate_x_w[j]), _row(rec_gate_a_b[j]), _row(rec_gate_x_b[j]),
                   _row(rec_lambda[j]), cf_conv_w[j], _row(cf_conv_b[j]), _row(cf_norm_g[j]), _row(cf_norm_b[j]),
                   w_out_ab[j].astype(BF16))
            yp, rc, hl, cf = _mixer_ab_call(
                yp, jnp.zeros((bp, REC_CONV - 1, d_rec), F32), jnp.zeros((bp, d_rec), F32),
                jnp.zeros((bp, CONV_WIDTH - 1, d_conv), F32), prm, ln_g, ln_b, alpha, tt_ab)
            p_rc.append(rc), p_h.append(hl), p_cf.append(cf)
            ys, rc, hl, cf = _mixer_ab_call(
                ys, state_rec_conv[j], state_rec_h[j], state_cf_conv[j], prm, ln_g, ln_b, alpha, tt_s)
            s_rc.append(rc), s_h.append(hl), s_cf.append(cf)
        else:
            wqkv, w_out = w_qkv[j].astype(BF16), w_out_c[j].astype(BF16)
            sinks = attn_sinks[j].astype(F32)
            d_q = N_HEADS * HEAD_DIM
            wq_t = wqkv[:, 0:d_q].T
            wk, wv = wqkv[:, d_q:d_q + d_kv], wqkv[:, d_q + d_kv:d_q + 2 * d_kv]
            yp, kk, vv = _mixer_c_prompt_call(yp, wq_t, wk, wv, wv.T, sinks, w_out, ln_g, ln_b, alpha, tt_c)
            p_k.append(kk.reshape(bp, WINDOW, N_KV, HEAD_DIM)), p_v.append(vv.reshape(bp, WINDOW, N_KV, HEAD_DIM))
            ck = cache_k[j].reshape(bs, WINDOW, d_kv)
            cv = cache_v[j].reshape(bs, WINDOW, d_kv)
            ys, kk, vv = _mixer_c_call(ys, ck, cv, wqkv, sinks, w_out, ln_g, ln_b, alpha, tt_s, False)
            s_k.append(kk.reshape(bs, WINDOW, N_KV, HEAD_DIM)), s_v.append(vv.reshape(bs, WINDOW, N_KV, HEAD_DIM))
        wg, wu, wd = _ffn_chunks(w_ff_gate[layer], w_ff_up[layer], w_ff_down[layer])
        fg, fb = _row(ln_ff_g[layer]), _row(ln_ff_b[layer])
        yp = _ffn(yp, wg, wu, wd, fg, fb, alpha, tm_p)
        ys = _ffn(ys, wg, wu, wd, fg, fb, alpha, tm_s)
    return (yp, ys, jnp.stack(p_h), jnp.stack(s_h), jnp.stack(p_rc), jnp.stack(s_rc),
            jnp.stack(p_cf), jnp.stack(s_cf), jnp.stack(p_k), jnp.stack(s_k), jnp.stack(p_v), jnp.stack(s_v))
```

```python
import functools

import jax
import jax.numpy as jnp
from jax import lax
from jax.experimental import pallas as pl
from jax.experimental.pallas import tpu as pltpu

CHUNK = 64
WINDOW = 128
HEAD_DIM = 64
N_HEADS = 16
N_KV = 4
GROUP = N_HEADS // N_KV
REC_CONV = 4
CONV_WIDTH = 31
LRU_C = 8.0
LN_EPS = 1e-5
NEG_INF = -1e30

SUBLANES = 8
MXU_DIM = 256
CONV_ROWS = 32
VMEM_LIMIT_BYTES = 56 * 1024 * 1024

BF16 = jnp.bfloat16
F32 = jnp.float32


def _dot(a, b):
    return jnp.dot(a, b, preferred_element_type=F32)


def _layer_norm(z, g, b):
    mu = jnp.mean(z, axis=-1, keepdims=True)
    zc = z - mu
    var = jnp.mean(zc * zc, axis=-1, keepdims=True)
    return zc * lax.rsqrt(var + LN_EPS) * g + b


def _sigmoid(x):
    return jax.nn.sigmoid(x)


def _mixer_ab_body(alpha, x_ref, rc0_ref, h0_ref, cf0_ref, w_in_ref, rcw_ref, rcb_ref, wgate_ref,
                   gab_ref, gxb_ref, lam_ref, cfw_ref, cfb_ref, cfg_ref, cfbeta_ref, w_out_ref,
                   lng_ref, lnb_ref,
                   y_ref, rc_out_ref, h_out_ref, cf_out_ref,
                   xr_buf, g_buf, a_buf, b_buf, h_buf, hcar, shift_buf, c_buf):
    tt = x_ref.shape[0]
    d_rec = rcw_ref.shape[1]
    d_conv = cfw_ref.shape[1]
    rc_tail = REC_CONV - 1
    cf_tail = CONV_WIDTH - 1
    rc_base = SUBLANES - rc_tail
    cf_base = 32 - cf_tail
    t = pl.program_id(1)

    @pl.when(t == 0)
    def _():
        xr_buf[rc_base:SUBLANES, :] = rc0_ref[...]
        g_buf[cf_base:32, :] = cf0_ref[...]
        hcar[...] = jnp.broadcast_to(h0_ref[...], hcar.shape)

    x = x_ref[...]
    xb = x.astype(BF16)

    xr = _dot(xb, w_in_ref[:, 0:d_rec])
    yr = _dot(xb, w_in_ref[:, d_rec:2 * d_rec])
    cv = _dot(xb, w_in_ref[:, 2 * d_rec:2 * d_rec + d_conv])
    cg = _dot(xb, w_in_ref[:, 2 * d_rec + d_conv:2 * d_rec + 2 * d_conv])

    xr_buf[SUBLANES:SUBLANES + tt, :] = xr
    xc = rcb_ref[...] + rcw_ref[rc_tail:rc_tail + 1, :] * xr
    for k in range(rc_tail):
        xc = xc + rcw_ref[k:k + 1, :] * xr_buf[rc_base + k:rc_base + k + tt, :]
    xcb = xc.astype(BF16)
    r_parts, i_parts = [], []
    for j in range(d_rec // MXU_DIM):
        lo, hi = j * MXU_DIM, (j + 1) * MXU_DIM
        gj = _dot(xcb[:, lo:hi], wgate_ref[j])
        r_parts.append(_sigmoid(gj[:, :MXU_DIM] + gab_ref[:, lo:hi]))
        i_parts.append(_sigmoid(gj[:, MXU_DIM:] + gxb_ref[:, lo:hi]))
    r = jnp.concatenate(r_parts, axis=1)
    i = jnp.concatenate(i_parts, axis=1)
    nlam = -lam_ref[...]
    softplus = jnp.maximum(nlam, 0.0) + jnp.log1p(jnp.exp(-jnp.abs(nlam)))
    log_a = (-LRU_C * softplus) * r
    a = jnp.exp(log_a)
    bu = jnp.sqrt(1.0 - a * a) * (i * xc)

    groups = tt // SUBLANES
    a3 = a.reshape(groups, SUBLANES, d_rec)
    b3 = bu.reshape(groups, SUBLANES, d_rec)
    sub = lax.broadcasted_iota(jnp.int32, a3.shape, 1)
    d = 1
    while d < SUBLANES:
        keep = sub >= d
        a_prev = jnp.where(keep, pltpu.roll(a3, d, axis=1), 1.0)
        b_prev = jnp.where(keep, pltpu.roll(b3, d, axis=1), 0.0)
        b3 = a3 * b_prev + b3
        a3 = a3 * a_prev
        d *= 2
    a_buf[...] = a3.reshape(tt, d_rec)
    b_buf[...] = b3.reshape(tt, d_rec)

    h = hcar[...]
    for j in range(groups):
        rows = slice(j * SUBLANES, (j + 1) * SUBLANES)
        hj = a_buf[rows, :] * h + b_buf[rows, :]
        h_buf[rows, :] = hj
        h = jnp.broadcast_to(hj[SUBLANES - 1:SUBLANES, :], hj.shape)
    hcar[...] = h

    rec_out = h_buf[...] * jax.nn.gelu(yr)

    gl = cv * _sigmoid(cg)
    g_buf[32:32 + tt, :] = gl
    n_ext = tt + 32
    g_ext = g_buf[...]
    for r in range(1, SUBLANES):
        shift_buf[r - 1] = pltpu.roll(g_ext, n_ext - r, axis=0)

    def conv_block(rb, carry):
        row0 = pl.multiple_of(rb * CONV_ROWS, CONV_ROWS)
        acc = jnp.broadcast_to(cfb_ref[...], (CONV_ROWS, d_conv))
        for k in range(CONV_WIDTH):
            q, r = divmod(cf_base + k, SUBLANES)
            if r == 0:
                src = g_buf[pl.ds(row0 + q * SUBLANES, CONV_ROWS), :]
            else:
                src = shift_buf[r - 1, pl.ds(row0 + q * SUBLANES, CONV_ROWS), :]
            acc = acc + cfw_ref[k:k + 1, :] * src
        c_buf[pl.ds(row0, CONV_ROWS), :] = acc
        return carry

    lax.fori_loop(0, tt // CONV_ROWS, conv_block, 0, unroll=True)
    cn = _layer_norm(c_buf[...], cfg_ref[...], cfbeta_ref[...])
    c2 = cn * _sigmoid(cn)

    out = _dot(rec_out.astype(BF16), w_out_ref[0:d_rec, :]) + _dot(c2.astype(BF16), w_out_ref[d_rec:d_rec + d_conv, :])
    y_ref[...] = _layer_norm(alpha * x + out, lng_ref[...], lnb_ref[...])

    new_rc = xr_buf[SUBLANES + tt - rc_tail:SUBLANES + tt, :]
    new_cf = g_buf[32 + tt - cf_tail:32 + tt, :]
    xr_buf[rc_base:SUBLANES, :] = new_rc
    g_buf[cf_base:32, :] = new_cf
    rc_out_ref[...] = new_rc
    cf_out_ref[...] = new_cf
    h_out_ref[...] = h[0:1, :]


def _const_spec(arr):
    zeros = (0,) * arr.ndim
    return pl.BlockSpec(arr.shape, lambda b, t: zeros)


def _mixer_ab_call(x, rc0, h0, cf0, prm, ln_g, ln_b, alpha, tt):
    bsz, seq, d_model = x.shape
    (w_in, rcw, rcb, wgate, gab, gxb, lam, cfw, cfb, cfg, cfbeta, w_out) = prm
    d_rec, d_conv = rcw.shape[1], cfw.shape[1]
    assert seq % tt == 0 and tt % CONV_ROWS == 0 and tt >= CONV_WIDTH - 1
    grid = (bsz, seq // tt)
    h0 = h0.reshape(bsz, 1, d_rec)
    consts = (w_in, rcw, rcb, wgate, gab, gxb, lam, cfw, cfb, cfg, cfbeta, w_out, ln_g, ln_b)
    in_specs = [
        pl.BlockSpec((None, tt, d_model), lambda b, t: (b, t, 0)),
        pl.BlockSpec((None, REC_CONV - 1, d_rec), lambda b, t: (b, 0, 0)),
        pl.BlockSpec((None, 1, d_rec), lambda b, t: (b, 0, 0)),
        pl.BlockSpec((None, CONV_WIDTH - 1, d_conv), lambda b, t: (b, 0, 0)),
    ] + [_const_spec(c) for c in consts]
    out_shape = (
        jax.ShapeDtypeStruct((bsz, seq, d_model), F32),
        jax.ShapeDtypeStruct((bsz, REC_CONV - 1, d_rec), F32),
        jax.ShapeDtypeStruct((bsz, 1, d_rec), F32),
        jax.ShapeDtypeStruct((bsz, CONV_WIDTH - 1, d_conv), F32),
    )
    out_specs = (
        pl.BlockSpec((None, tt, d_model), lambda b, t: (b, t, 0)),
        pl.BlockSpec((None, REC_CONV - 1, d_rec), lambda b, t: (b, 0, 0)),
        pl.BlockSpec((None, 1, d_rec), lambda b, t: (b, 0, 0)),
        pl.BlockSpec((None, CONV_WIDTH - 1, d_conv), lambda b, t: (b, 0, 0)),
    )
    scratch = [
        pltpu.VMEM((tt + SUBLANES, d_rec), F32),
        pltpu.VMEM((tt + 32, d_conv), F32),
        pltpu.VMEM((tt, d_rec), F32),
        pltpu.VMEM((tt, d_rec), F32),
        pltpu.VMEM((tt, d_rec), F32),
        pltpu.VMEM((SUBLANES, d_rec), F32),
        pltpu.VMEM((SUBLANES - 1, tt + 32, d_conv), F32),
        pltpu.VMEM((tt, d_conv), F32),
    ]
    y, rc, h, cf = pl.pallas_call(
        functools.partial(_mixer_ab_body, alpha),
        grid=grid, in_specs=in_specs, out_specs=out_specs, out_shape=out_shape,
        scratch_shapes=scratch,
        compiler_params=pltpu.CompilerParams(
            dimension_semantics=("arbitrary", "arbitrary"), vmem_limit_bytes=VMEM_LIMIT_BYTES),
        name="mixer_ab",
    )(x, rc0, h0, cf0, *consts)
    return y, rc, h.reshape(bsz, d_rec), cf


def _mixer_c_body(alpha, masked, sink_ref, x_ref, k0_ref, v0_ref, wqkv_ref, w_out_ref, lng_ref, lnb_ref,
                  y_ref, k_out_ref, v_out_ref,
                  q_buf, k_buf, v_buf, o_buf, bias_buf, sink_buf):
    tt = x_ref.shape[0]
    d_q = N_HEADS * HEAD_DIM
    d_kv = N_KV * HEAD_DIM
    win_keys = WINDOW + CHUNK
    stack = GROUP * CHUNK
    t = pl.program_id(1)

    @pl.when(t == 0)
    def _():
        k_buf[0:WINDOW, :] = k0_ref[...].astype(BF16)
        v_buf[0:WINDOW, :] = v0_ref[...].astype(BF16)
        row = lax.broadcasted_iota(jnp.int32, (stack, win_keys), 0)
        ko = lax.broadcasted_iota(jnp.int32, (stack, win_keys), 1)
        absdist = jnp.abs(WINDOW + (row % CHUNK) - ko).astype(F32)
        row1 = lax.broadcasted_iota(jnp.int32, (stack, 1), 0)
        for kh in range(N_KV):
            slope = jnp.zeros((stack, win_keys), F32)
            sink = jnp.zeros((stack, 1), F32)
            for g in range(GROUP):
                h = kh * GROUP + g
                slope = jnp.where(row // CHUNK == g, 2.0 ** (-8.0 * (h + 1) / N_HEADS), slope)
                sink = jnp.where(row1 // CHUNK == g, sink_ref[h], sink)
            bias_buf[kh] = -slope * absdist
            sink_buf[kh] = sink

    x = x_ref[...]
    xb = x.astype(BF16)
    q_buf[...] = (_dot(xb, wqkv_ref[:, 0:d_q]) * (HEAD_DIM ** -0.5)).astype(BF16)
    k = _dot(xb, wqkv_ref[:, d_q:d_q + d_kv])
    v = _dot(xb, wqkv_ref[:, d_q + d_kv:d_q + 2 * d_kv])
    k_buf[WINDOW:WINDOW + tt, :] = k.astype(BF16)
    v_buf[WINDOW:WINDOW + tt, :] = v.astype(BF16)
    if tt >= WINDOW:
        k_out_ref[...] = k[tt - WINDOW:tt, :]
        v_out_ref[...] = v[tt - WINDOW:tt, :]
    else:
        k_out_ref[0:WINDOW - tt, :] = k0_ref[tt:WINDOW, :]
        v_out_ref[0:WINDOW - tt, :] = v0_ref[tt:WINDOW, :]
        k_out_ref[WINDOW - tt:WINDOW, :] = k
        v_out_ref[WINDOW - tt:WINDOW, :] = v

    def attend(r0, mask_keys):
        scores = []
        for kh in range(N_KV):
            q4 = q_buf[pl.ds(r0, CHUNK), kh * stack:(kh + 1) * stack]
            qs = jnp.concatenate([q4[:, g * HEAD_DIM:(g + 1) * HEAD_DIM] for g in range(GROUP)], axis=0)
            kwh = k_buf[pl.ds(r0, win_keys), kh * HEAD_DIM:(kh + 1) * HEAD_DIM]
            s = lax.dot_general(qs, kwh, (((1,), (1,)), ((), ())), preferred_element_type=F32)
            s = s + bias_buf[kh]
            if mask_keys:
                ko = lax.broadcasted_iota(jnp.int32, (stack, win_keys), 1)
                s = jnp.where((t * tt + r0 - WINDOW + ko) >= 0, s, NEG_INF)
            scores.append(s)
        probs, denoms = [], []
        for kh in range(N_KV):
            s = scores[kh]
            sink = sink_buf[kh]
            m = jnp.maximum(jnp.max(s, axis=-1, keepdims=True), sink)
            p = jnp.exp(s - m)
            denoms.append(jnp.sum(p, axis=-1, keepdims=True) + jnp.exp(sink - m))
            probs.append(p.astype(BF16))
        for kh in range(N_KV):
            vwh = v_buf[pl.ds(r0, win_keys), kh * HEAD_DIM:(kh + 1) * HEAD_DIM]
            o = (_dot(probs[kh], vwh) / denoms[kh]).astype(BF16)
            for g in range(GROUP):
                h = kh * GROUP + g
                o_buf[pl.ds(r0, CHUNK), h * HEAD_DIM:(h + 1) * HEAD_DIM] = o[g * CHUNK:(g + 1) * CHUNK, :]

    def chunk_body(c, carry):
        r0 = pl.multiple_of(c * CHUNK, CHUNK)
        if masked:
            lax.cond(t * tt + r0 < WINDOW, lambda: attend(r0, True), lambda: attend(r0, False))
        else:
            attend(r0, False)
        return carry

    lax.fori_loop(0, tt // CHUNK, chunk_body, 0)

    out = _dot(o_buf[...], w_out_ref[...])
    y_ref[...] = _layer_norm(alpha * x + out, lng_ref[...], lnb_ref[...])
    k_buf[0:WINDOW, :] = k_buf[tt:tt + WINDOW, :]
    v_buf[0:WINDOW, :] = v_buf[tt:tt + WINDOW, :]


def _mixer_c_call(x, k0, v0, wqkv, sinks, w_out, ln_g, ln_b, alpha, tt, masked):
    bsz, seq, d_model = x.shape
    d_q = N_HEADS * HEAD_DIM
    d_kv = N_KV * HEAD_DIM
    assert seq % tt == 0 and tt % CHUNK == 0
    assert tt >= WINDOW or seq == tt
    grid = (bsz, seq // tt)
    consts = (wqkv, w_out, ln_g, ln_b)
    in_specs = [
        pl.BlockSpec(memory_space=pltpu.SMEM),
        pl.BlockSpec((None, tt, d_model), lambda b, t: (b, t, 0)),
        pl.BlockSpec((None, WINDOW, d_kv), lambda b, t: (b, 0, 0)),
        pl.BlockSpec((None, WINDOW, d_kv), lambda b, t: (b, 0, 0)),
    ] + [_const_spec(c) for c in consts]
    out_shape = (
        jax.ShapeDtypeStruct((bsz, seq, d_model), F32),
        jax.ShapeDtypeStruct((bsz, WINDOW, d_kv), F32),
        jax.ShapeDtypeStruct((bsz, WINDOW, d_kv), F32),
    )
    out_specs = (
        pl.BlockSpec((None, tt, d_model), lambda b, t: (b, t, 0)),
        pl.BlockSpec((None, WINDOW, d_kv), lambda b, t: (b, 0, 0)),
        pl.BlockSpec((None, WINDOW, d_kv), lambda b, t: (b, 0, 0)),
    )
    scratch = [
        pltpu.VMEM((tt, d_q), BF16),
        pltpu.VMEM((tt + WINDOW, d_kv), BF16),
        pltpu.VMEM((tt + WINDOW, d_kv), BF16),
        pltpu.VMEM((tt, d_q), BF16),
        pltpu.VMEM((N_KV, GROUP * CHUNK, WINDOW + CHUNK), F32),
        pltpu.VMEM((N_KV, GROUP * CHUNK, 1), F32),
    ]
    return pl.pallas_call(
        functools.partial(_mixer_c_body, alpha, masked),
        grid=grid, in_specs=in_specs, out_specs=out_specs, out_shape=out_shape,
        scratch_shapes=scratch,
        compiler_params=pltpu.CompilerParams(
            dimension_semantics=("arbitrary", "arbitrary"), vmem_limit_bytes=VMEM_LIMIT_BYTES),
        name="mixer_c",
    )(sinks, x, k0, v0, *consts)


QBLK = 2 * CHUNK
KBLK = WINDOW + QBLK
ONES_ROWS = 16
ATTN_SKEW = 3


def _mixer_c_prompt_body(alpha, sink_ref, x_ref, wqT_ref, wk_ref, wv_ref, wvT_ref, w_out_ref, lng_ref, lnb_ref,
                         y_ref, k_out_ref, v_out_ref,
                         qT_buf, k_buf, vT_buf, oT_buf, bias_buf, sink_buf):
    tt = x_ref.shape[0]
    t = pl.program_id(1)
    n_pairs = N_HEADS // 2
    lanes = 2 * QBLK

    @pl.when(t == 0)
    def _():
        vT_buf[...] = jnp.ones(vT_buf.shape, BF16)
        k_buf[:, 0:WINDOW, :] = jnp.zeros((N_KV, WINDOW, HEAD_DIM), BF16)
        kr = lax.broadcasted_iota(jnp.int32, (KBLK, lanes), 0)
        ln = lax.broadcasted_iota(jnp.int32, (KBLK, lanes), 1)
        ql = ln % QBLK
        absdist = jnp.abs(ql + WINDOW - kr).astype(F32)
        kc, qc = kr // CHUNK, ql // CHUNK
        in_window = (kc >= qc) & (kc <= qc + WINDOW // CHUNK)
        ln1 = lax.broadcasted_iota(jnp.int32, (SUBLANES, lanes), 1)
        for pr in range(n_pairs):
            s0 = 2.0 ** (-8.0 * (2 * pr + 1) / N_HEADS)
            s1 = 2.0 ** (-8.0 * (2 * pr + 2) / N_HEADS)
            slope = jnp.where(ln < QBLK, s0, s1)
            bias_buf[pr] = jnp.where(in_window, -slope * absdist, NEG_INF)
            sink_buf[pr] = jnp.where(ln1 < QBLK, sink_ref[2 * pr], sink_ref[2 * pr + 1])

    x = x_ref[...]
    xb = x.astype(BF16)
    nt_dims = (((1,), (1,)), ((), ()))
    qT = lax.dot_general(wqT_ref[...], xb, nt_dims, preferred_element_type=F32)
    qT_buf[...] = (qT * (HEAD_DIM ** -0.5)).astype(BF16)
    k = _dot(xb, wk_ref[...])
    kb = k.astype(BF16)
    vT = lax.dot_general(wvT_ref[...], xb, nt_dims, preferred_element_type=F32)
    for kh in range(N_KV):
        k_buf[kh, WINDOW:WINDOW + tt, :] = kb[:, kh * HEAD_DIM:(kh + 1) * HEAD_DIM]
        vT_buf[kh, 0:HEAD_DIM, WINDOW:WINDOW + tt] = vT[kh * HEAD_DIM:(kh + 1) * HEAD_DIM, :].astype(BF16)
    k_out_ref[...] = k[tt - WINDOW:tt, :]

    @pl.when(t == pl.num_programs(1) - 1)
    def _():
        v_out_ref[...] = _dot(xb[tt - WINDOW:tt, :], wv_ref[...])

    def scores(qb, pr, key_lo):
        kh = pr // (GROUP // 2)
        q_lanes = slice(qb * QBLK, (qb + 1) * QBLK)
        h_rows = [slice((2 * pr + i) * HEAD_DIM, (2 * pr + i + 1) * HEAD_DIM) for i in range(2)]
        q2 = jnp.concatenate([qT_buf[h_rows[0], q_lanes], qT_buf[h_rows[1], q_lanes]], axis=1)
        keys = slice(qb * QBLK + key_lo, qb * QBLK + KBLK)
        return _dot(k_buf[kh, keys, :], q2) + bias_buf[pr, key_lo:KBLK, :]

    def finish(qb, pr, key_lo, sT):
        kh = pr // (GROUP // 2)
        q_lanes = slice(qb * QBLK, (qb + 1) * QBLK)
        h_rows = [slice((2 * pr + i) * HEAD_DIM, (2 * pr + i + 1) * HEAD_DIM) for i in range(2)]
        keys = slice(qb * QBLK + key_lo, qb * QBLK + KBLK)
        sink = sink_buf[pr][0:1, :]
        m = jnp.maximum(jnp.max(sT, axis=0, keepdims=True), sink)
        p = jnp.exp(sT - m)
        pv = _dot(vT_buf[kh, :, keys], p.astype(BF16))
        den = pv[HEAD_DIM:HEAD_DIM + 1, :] + jnp.exp(sink - m)
        oT = (pv[0:HEAD_DIM, :] * (1.0 / den)).astype(BF16)
        oT_buf[h_rows[0], q_lanes] = oT[:, 0:QBLK]
        oT_buf[h_rows[1], q_lanes] = oT[:, QBLK:lanes]

    def attend_all(first_key_lo):
        items = [(qb, pr, first_key_lo if qb == 0 else 0) for qb in range(tt // QBLK) for pr in range(n_pairs)]
        pending = []
        for item in items:
            pending.append((item, scores(*item)))
            if len(pending) > ATTN_SKEW:
                done, sT = pending.pop(0)
                finish(*done, sT)
        for done, sT in pending:
            finish(*done, sT)

    @pl.when(t == 0)
    def _():
        attend_all(WINDOW)

    @pl.when(t != 0)
    def _():
        attend_all(0)

    out = lax.dot_general(oT_buf[...], w_out_ref[...], (((0,), (0,)), ((), ())), preferred_element_type=F32)
    y_ref[...] = _layer_norm(alpha * x + out, lng_ref[...], lnb_ref[...])
    k_buf[:, 0:WINDOW, :] = k_buf[:, tt:tt + WINDOW, :]
    vT_buf[:, 0:HEAD_DIM, 0:WINDOW] = vT_buf[:, 0:HEAD_DIM, tt:tt + WINDOW]


def _mixer_c_prompt_call(x, wqT, wk, wv, wvT, sinks, w_out, ln_g, ln_b, alpha, tt):
    bsz, seq, d_model = x.shape
    d_q = N_HEADS * HEAD_DIM
    d_kv = N_KV * HEAD_DIM
    assert seq % tt == 0 and tt % QBLK == 0 and tt >= WINDOW
    consts = (wqT, wk, wv, wvT, w_out, ln_g, ln_b)
    in_specs = [
        pl.BlockSpec(memory_space=pltpu.SMEM),
        pl.BlockSpec((None, tt, d_model), lambda b, t: (b, t, 0)),
    ] + [_const_spec(c) for c in consts]
    out_shape = (
        jax.ShapeDtypeStruct((bsz, seq, d_model), F32),
        jax.ShapeDtypeStruct((bsz, WINDOW, d_kv), F32),
        jax.ShapeDtypeStruct((bsz, WINDOW, d_kv), F32),
    )
    out_specs = (
        pl.BlockSpec((None, tt, d_model), lambda b, t: (b, t, 0)),
        pl.BlockSpec((None, WINDOW, d_kv), lambda b, t: (b, 0, 0)),
        pl.BlockSpec((None, WINDOW, d_kv), lambda b, t: (b, 0, 0)),
    )
    scratch = [
        pltpu.VMEM((d_q, tt), BF16),
        pltpu.VMEM((N_KV, tt + WINDOW, HEAD_DIM), BF16),
        pltpu.VMEM((N_KV, HEAD_DIM + ONES_ROWS, tt + WINDOW), BF16),
        pltpu.VMEM((d_q, tt), BF16),
        pltpu.VMEM((N_HEADS // 2, KBLK, 2 * QBLK), F32),
        pltpu.VMEM((N_HEADS // 2, SUBLANES, 2 * QBLK), F32),
    ]
    return pl.pallas_call(
        functools.partial(_mixer_c_prompt_body, alpha),
        grid=(bsz, seq // tt), in_specs=in_specs, out_specs=out_specs, out_shape=out_shape,
        scratch_shapes=scratch,
        compiler_params=pltpu.CompilerParams(
            dimension_semantics=("arbitrary", "arbitrary"), vmem_limit_bytes=VMEM_LIMIT_BYTES),
        name="mixer_c_prompt",
    )(sinks, x, *consts)


FFN_UNROLL = 5


def _ffn_body(alpha, x_ref, wg_ref, wu_ref, wd_ref, lng_ref, lnb_ref, y_ref, xb_buf, acc_buf):
    n_chunks = wg_ref.shape[0]
    xb_buf[...] = x_ref[...].astype(BF16)

    def down(c):
        xb = xb_buf[...]
        g = _dot(xb, wg_ref[c])
        u = _dot(xb, wu_ref[c])
        hid = (g * _sigmoid(g) * u).astype(BF16)
        return _dot(hid, wd_ref[c])

    acc_buf[...] = down(0)

    def chunk_body(c, carry):
        acc_buf[...] += down(c)
        return carry

    lax.fori_loop(1, n_chunks, chunk_body, 0, unroll=FFN_UNROLL)
    y_ref[...] = _layer_norm(alpha * x_ref[...] + acc_buf[...], lng_ref[...], lnb_ref[...])


def _ffn_call(x2d, wg, wu, wd, ln_g, ln_b, alpha, tm):
    rows, d_model = x2d.shape
    assert rows % tm == 0
    consts = (wg, wu, wd, ln_g, ln_b)

    def const_spec(arr):
        zeros = (0,) * arr.ndim
        return pl.BlockSpec(arr.shape, lambda i: zeros, pipeline_mode=pl.Buffered(1))

    return pl.pallas_call(
        functools.partial(_ffn_body, alpha),
        grid=(rows // tm,),
        in_specs=[pl.BlockSpec((tm, d_model), lambda i: (i, 0))] + [const_spec(c) for c in consts],
        out_specs=pl.BlockSpec((tm, d_model), lambda i: (i, 0)),
        out_shape=jax.ShapeDtypeStruct((rows, d_model), F32),
        scratch_shapes=[pltpu.VMEM((tm, d_model), BF16), pltpu.VMEM((tm, d_model), F32)],
        compiler_params=pltpu.CompilerParams(
            dimension_semantics=("arbitrary",), vmem_limit_bytes=VMEM_LIMIT_BYTES),
        name="ffn",
    )(x2d, *consts)


def _ffn(x, wg, wu, wd, ln_g, ln_b, alpha, tm):
    bsz, seq, d_model = x.shape
    return _ffn_call(x.reshape(bsz * seq, d_model), wg, wu, wd, ln_g, ln_b, alpha, tm).reshape(x.shape)


def _row(v):
    return v.reshape(1, -1).astype(F32)


def _gate_tiles(ga_w, gx_w):
    heads, blk, _ = ga_w.shape
    per_tile = MXU_DIM // blk
    tiles = []
    for j in range(heads // per_tile):
        def diag(w):
            return jax.scipy.linalg.block_diag(*[w[j * per_tile + i] for i in range(per_tile)])
        tiles.append(jnp.concatenate([diag(ga_w), diag(gx_w)], axis=1))
    return jnp.stack(tiles).astype(BF16)


def _ffn_chunks(wg, wu, wd):
    d_model, d_ff = wg.shape
    n = d_ff // MXU_DIM
    wg_c = wg.reshape(d_model, n, MXU_DIM).transpose(1, 0, 2).astype(BF16)
    wu_c = wu.reshape(d_model, n, MXU_DIM).transpose(1, 0, 2).astype(BF16)
    wd_c = wd.reshape(n, MXU_DIM, d_model).astype(BF16)
    return wg_c, wu_c, wd_c


def kernel(x_prompt, x_sample, state_rec_h, state_rec_conv, state_cf_conv, cache_k, cache_v, w_in_ab, rec_conv_w, rec_conv_b, rec_gate_a_w, rec_gate_a_b, rec_gate_x_w, rec_gate_x_b, rec_lambda, cf_conv_w, cf_conv_b, cf_norm_g, cf_norm_b, w_out_ab, w_qkv, attn_sinks, w_out_c, ln_mix_g, ln_mix_b, w_ff_gate, w_ff_up, w_ff_down, ln_ff_g, ln_ff_b):
    depth = ln_mix_g.shape[0]
    alpha = (2 * depth) ** 0.25
    bp, seq_p, d_model = x_prompt.shape
    bs, seq_s, _ = x_sample.shape
    d_rec = rec_conv_w.shape[-1]
    d_conv = cf_conv_w.shape[-1]
    d_kv = N_KV * HEAD_DIM
    tt_ab = min(seq_p, 512)
    tt_c = min(seq_p, 512)
    tt_s = seq_s
    tm_p = min(bp * seq_p, 1024)
    tm_s = bs * seq_s

    yp, ys = x_prompt, x_sample
    p_h, s_h, p_rc, s_rc, p_cf, s_cf, p_k, s_k, p_v, s_v = ([] for _ in range(10))
    for layer in range(depth):
        j = layer // 2
        ln_g, ln_b = _row(ln_mix_g[layer]), _row(ln_mix_b[layer])
        if layer % 2 == 0:
            prm = (w_in_ab[j].astype(BF16), rec_conv_w[j], _row(rec_conv_b[j]),
                   _gate_tiles(rec_gate_a_w[j], rec_gate_x_w[j]), _row(rec_gate_a_b[j]), _row(rec_gate_x_b[j]),
                   _row(rec_lambda[j]), cf_conv_w[j], _row(cf_conv_b[j]), _row(cf_norm_g[j]), _row(cf_norm_b[j]),
                   w_out_ab[j].astype(BF16))
            yp, rc, hl, cf = _mixer_ab_call(
                yp, jnp.zeros((bp, REC_CONV - 1, d_rec), F32), jnp.zeros((bp, d_rec), F32),
                jnp.zeros((bp, CONV_WIDTH - 1, d_conv), F32), prm, ln_g, ln_b, alpha, tt_ab)
            p_rc.append(rc), p_h.append(hl), p_cf.append(cf)
            ys, rc, hl, cf = _mixer_ab_call(
                ys, state_rec_conv[j], state_rec_h[j], state_cf_conv[j], prm, ln_g, ln_b, alpha, tt_s)
            s_rc.append(rc), s_h.append(hl), s_cf.append(cf)
        else:
            wqkv, w_out = w_qkv[j].astype(BF16), w_out_c[j].astype(BF16)
            sinks = attn_sinks[j].astype(F32)
            d_q = N_HEADS * HEAD_DIM
            wq_t = wqkv[:, 0:d_q].T
            wk, wv = wqkv[:, d_q:d_q + d_kv], wqkv[:, d_q + d_kv:d_q + 2 * d_kv]
            yp, kk, vv = _mixer_c_prompt_call(yp, wq_t, wk, wv, wv.T, sinks, w_out, ln_g, ln_b, alpha, tt_c)
            p_k.append(kk.reshape(bp, WINDOW, N_KV, HEAD_DIM)), p_v.append(vv.reshape(bp, WINDOW, N_KV, HEAD_DIM))
            ck = cache_k[j].reshape(bs, WINDOW, d_kv)
            cv = cache_v[j].reshape(bs, WINDOW, d_kv)
            ys, kk, vv = _mixer_c_call(ys, ck, cv, wqkv, sinks, w_out, ln_g, ln_b, alpha, tt_s, False)
            s_k.append(kk.reshape(bs, WINDOW, N_KV, HEAD_DIM)), s_v.append(vv.reshape(bs, WINDOW, N_KV, HEAD_DIM))
        wg, wu, wd = _ffn_chunks(w_ff_gate[layer], w_ff_up[layer], w_ff_down[layer])
        fg, fb = _row(ln_ff_g[layer]), _row(ln_ff_b[layer])
        yp = _ffn(yp, wg, wu, wd, fg, fb, alpha, tm_p)
        ys = _ffn(ys, wg, wu, wd, fg, fb, alpha, tm_s)
    return (yp, ys, jnp.stack(p_h), jnp.stack(s_h), jnp.stack(p_rc), jnp.stack(s_rc),
            jnp.stack(p_cf), jnp.stack(s_cf), jnp.stack(p_k), jnp.stack(s_k), jnp.stack(p_v), jnp.stack(s_v))
```

```python
import functools

import jax
import jax.numpy as jnp
from jax import lax
from jax.experimental import pallas as pl
from jax.experimental.pallas import tpu as pltpu

CHUNK = 64
WINDOW = 128
HEAD_DIM = 64
N_HEADS = 16
N_KV = 4
GROUP = N_HEADS // N_KV
REC_CONV = 4
CONV_WIDTH = 31
LRU_C = 8.0
LN_EPS = 1e-5
NEG_INF = -1e30

SUBLANES = 8
MXU_DIM = 256
CONV_ROWS = 32
VMEM_LIMIT_BYTES = 56 * 1024 * 1024

BF16 = jnp.bfloat16
F32 = jnp.float32


def _dot(a, b):
    return jnp.dot(a, b, preferred_element_type=F32)


def _layer_norm(z, g, b):
    mu = jnp.mean(z, axis=-1, keepdims=True)
    zc = z - mu
    var = jnp.mean(zc * zc, axis=-1, keepdims=True)
    return zc * lax.rsqrt(var + LN_EPS) * g + b


def _sigmoid(x):
    return jax.nn.sigmoid(x)


def _mixer_ab_body(alpha, x_ref, rc0_ref, h0_ref, cf0_ref, w_in_ref, rcw_ref, rcb_ref, wgate_ref,
                   gab_ref, gxb_ref, lam_ref, cfw_ref, cfb_ref, cfg_ref, cfbeta_ref, w_out_ref,
                   lng_ref, lnb_ref,
                   y_ref, rc_out_ref, h_out_ref, cf_out_ref,
                   xr_buf, g_buf, a_buf, b_buf, h_buf, hcar, shift_buf, c_buf, wb_buf):
    tt = x_ref.shape[0]
    d_rec = rcw_ref.shape[1]
    d_conv = cfw_ref.shape[1]
    rc_tail = REC_CONV - 1
    cf_tail = CONV_WIDTH - 1
    rc_base = SUBLANES - rc_tail
    cf_base = 32 - cf_tail
    t = pl.program_id(1)

    @pl.when(t == 0)
    def _():
        xr_buf[rc_base:SUBLANES, :] = rc0_ref[...]
        g_buf[cf_base:32, :] = cf0_ref[...]
        hcar[...] = jnp.broadcast_to(h0_ref[...], hcar.shape)
        for k in range(CONV_WIDTH):
            wb_buf[k] = jnp.broadcast_to(cfw_ref[k:k + 1, :], (SUBLANES, d_conv))

    x = x_ref[...]
    xb = x.astype(BF16)

    xr = _dot(xb, w_in_ref[:, 0:d_rec])
    yr = _dot(xb, w_in_ref[:, d_rec:2 * d_rec])
    cv = _dot(xb, w_in_ref[:, 2 * d_rec:2 * d_rec + d_conv])
    cg = _dot(xb, w_in_ref[:, 2 * d_rec + d_conv:2 * d_rec + 2 * d_conv])

    xr_buf[SUBLANES:SUBLANES + tt, :] = xr
    xc = rcb_ref[...] + rcw_ref[rc_tail:rc_tail + 1, :] * xr
    for k in range(rc_tail):
        xc = xc + rcw_ref[k:k + 1, :] * xr_buf[rc_base + k:rc_base + k + tt, :]
    xcb = xc.astype(BF16)
    r_parts, i_parts = [], []
    for j in range(d_rec // MXU_DIM):
        lo, hi = j * MXU_DIM, (j + 1) * MXU_DIM
        gj = _dot(xcb[:, lo:hi], wgate_ref[j])
        r_parts.append(_sigmoid(gj[:, :MXU_DIM] + gab_ref[:, lo:hi]))
        i_parts.append(_sigmoid(gj[:, MXU_DIM:] + gxb_ref[:, lo:hi]))
    r = jnp.concatenate(r_parts, axis=1)
    i = jnp.concatenate(i_parts, axis=1)
    nlam = -lam_ref[...]
    softplus = jnp.maximum(nlam, 0.0) + jnp.log1p(jnp.exp(-jnp.abs(nlam)))
    log_a = (-LRU_C * softplus) * r
    a = jnp.exp(log_a)
    y = 1.0 - a * a
    bu = jnp.where(y > 0.0, y * lax.rsqrt(y), 0.0) * (i * xc)

    groups = tt // SUBLANES
    a3 = a.reshape(groups, SUBLANES, d_rec)
    b3 = bu.reshape(groups, SUBLANES, d_rec)
    sub = lax.broadcasted_iota(jnp.int32, a3.shape, 1)
    d = 1
    while d < SUBLANES:
        keep = sub >= d
        a_prev = jnp.where(keep, pltpu.roll(a3, d, axis=1), 1.0)
        b_prev = jnp.where(keep, pltpu.roll(b3, d, axis=1), 0.0)
        b3 = a3 * b_prev + b3
        a3 = a3 * a_prev
        d *= 2
    a_buf[...] = a3.reshape(tt, d_rec)
    b_buf[...] = b3.reshape(tt, d_rec)

    h = hcar[...]
    for j in range(groups):
        rows = slice(j * SUBLANES, (j + 1) * SUBLANES)
        hj = a_buf[rows, :] * h + b_buf[rows, :]
        h_buf[rows, :] = hj
        h = jnp.broadcast_to(hj[SUBLANES - 1:SUBLANES, :], hj.shape)
    hcar[...] = h

    rec_out = h_buf[...] * jax.nn.gelu(yr)

    gl = cv * _sigmoid(cg)
    g_buf[32:32 + tt, :] = gl
    ext_groups = (tt + 32) // SUBLANES
    g_ext = g_buf[...].reshape(ext_groups, SUBLANES, d_conv)
    sub = lax.broadcasted_iota(jnp.int32, (ext_groups - 1, SUBLANES, d_conv), 1)
    for r in range(1, SUBLANES):
        rot = pltpu.roll(g_ext, SUBLANES - r, axis=1)
        shifted = jnp.where(sub < SUBLANES - r, rot[0:ext_groups - 1], rot[1:ext_groups])
        shift_buf[r - 1, 0:tt + 32 - SUBLANES, :] = shifted.reshape(tt + 32 - SUBLANES, d_conv)

    blk_groups = CONV_ROWS // SUBLANES

    def conv_block(rb, carry):
        row0 = pl.multiple_of(rb * CONV_ROWS, CONV_ROWS)
        acc = jnp.broadcast_to(cfb_ref[...], (CONV_ROWS, d_conv)).reshape(blk_groups, SUBLANES, d_conv)
        for k in range(CONV_WIDTH):
            q, r = divmod(cf_base + k, SUBLANES)
            if r == 0:
                src = g_buf[pl.ds(row0 + q * SUBLANES, CONV_ROWS), :]
            else:
                src = shift_buf[r - 1, pl.ds(row0 + q * SUBLANES, CONV_ROWS), :]
            acc = acc + wb_buf[k] * src.reshape(blk_groups, SUBLANES, d_conv)
        c_buf[pl.ds(row0, CONV_ROWS), :] = acc.reshape(CONV_ROWS, d_conv)
        return carry

    lax.fori_loop(0, tt // CONV_ROWS, conv_block, 0, unroll=True)
    cn = _layer_norm(c_buf[...], cfg_ref[...], cfbeta_ref[...])
    c2 = cn * _sigmoid(cn)

    out = _dot(rec_out.astype(BF16), w_out_ref[0:d_rec, :]) + _dot(c2.astype(BF16), w_out_ref[d_rec:d_rec + d_conv, :])
    y_ref[...] = _layer_norm(alpha * x + out, lng_ref[...], lnb_ref[...])

    new_rc = xr_buf[SUBLANES + tt - rc_tail:SUBLANES + tt, :]
    new_cf = g_buf[32 + tt - cf_tail:32 + tt, :]
    xr_buf[rc_base:SUBLANES, :] = new_rc
    g_buf[cf_base:32, :] = new_cf
    rc_out_ref[...] = new_rc
    cf_out_ref[...] = new_cf
    h_out_ref[...] = h[0:1, :]


def _const_spec(arr):
    zeros = (0,) * arr.ndim
    return pl.BlockSpec(arr.shape, lambda b, t: zeros)


def _mixer_ab_call(x, rc0, h0, cf0, prm, ln_g, ln_b, alpha, tt):
    bsz, seq, d_model = x.shape
    (w_in, rcw, rcb, wgate, gab, gxb, lam, cfw, cfb, cfg, cfbeta, w_out) = prm
    d_rec, d_conv = rcw.shape[1], cfw.shape[1]
    assert seq % tt == 0 and tt % CONV_ROWS == 0 and tt >= CONV_WIDTH - 1
    grid = (bsz, seq // tt)
    h0 = h0.reshape(bsz, 1, d_rec)
    consts = (w_in, rcw, rcb, wgate, gab, gxb, lam, cfw, cfb, cfg, cfbeta, w_out, ln_g, ln_b)
    in_specs = [
        pl.BlockSpec((None, tt, d_model), lambda b, t: (b, t, 0)),
        pl.BlockSpec((None, REC_CONV - 1, d_rec), lambda b, t: (b, 0, 0)),
        pl.BlockSpec((None, 1, d_rec), lambda b, t: (b, 0, 0)),
        pl.BlockSpec((None, CONV_WIDTH - 1, d_conv), lambda b, t: (b, 0, 0)),
    ] + [_const_spec(c) for c in consts]
    out_shape = (
        jax.ShapeDtypeStruct((bsz, seq, d_model), F32),
        jax.ShapeDtypeStruct((bsz, REC_CONV - 1, d_rec), F32),
        jax.ShapeDtypeStruct((bsz, 1, d_rec), F32),
        jax.ShapeDtypeStruct((bsz, CONV_WIDTH - 1, d_conv), F32),
    )
    out_specs = (
        pl.BlockSpec((None, tt, d_model), lambda b, t: (b, t, 0)),
        pl.BlockSpec((None, REC_CONV - 1, d_rec), lambda b, t: (b, 0, 0)),
        pl.BlockSpec((None, 1, d_rec), lambda b, t: (b, 0, 0)),
        pl.BlockSpec((None, CONV_WIDTH - 1, d_conv), lambda b, t: (b, 0, 0)),
    )
    scratch = [
        pltpu.VMEM((tt + SUBLANES, d_rec), F32),
        pltpu.VMEM((tt + 32, d_conv), F32),
        pltpu.VMEM((tt, d_rec), F32),
        pltpu.VMEM((tt, d_rec), F32),
        pltpu.VMEM((tt, d_rec), F32),
        pltpu.VMEM((SUBLANES, d_rec), F32),
        pltpu.VMEM((SUBLANES - 1, tt + 32, d_conv), F32),
        pltpu.VMEM((tt, d_conv), F32),
        pltpu.VMEM((CONV_WIDTH, SUBLANES, d_conv), F32),
    ]
    y, rc, h, cf = pl.pallas_call(
        functools.partial(_mixer_ab_body, alpha),
        grid=grid, in_specs=in_specs, out_specs=out_specs, out_shape=out_shape,
        scratch_shapes=scratch,
        compiler_params=pltpu.CompilerParams(
            dimension_semantics=("arbitrary", "arbitrary"), vmem_limit_bytes=VMEM_LIMIT_BYTES),
        name="mixer_ab",
    )(x, rc0, h0, cf0, *consts)
    return y, rc, h.reshape(bsz, d_rec), cf


def _mixer_c_body(alpha, masked, sink_ref, x_ref, k0_ref, v0_ref, wqkv_ref, w_out_ref, lng_ref, lnb_ref,
                  y_ref, k_out_ref, v_out_ref,
                  q_buf, k_buf, v_buf, o_buf, bias_buf, sink_buf):
    tt = x_ref.shape[0]
    d_q = N_HEADS * HEAD_DIM
    d_kv = N_KV * HEAD_DIM
    win_keys = WINDOW + CHUNK
    stack = GROUP * CHUNK
    t = pl.program_id(1)

    @pl.when(t == 0)
    def _():
        k_buf[0:WINDOW, :] = k0_ref[...].astype(BF16)
        v_buf[0:WINDOW, :] = v0_ref[...].astype(BF16)
        row = lax.broadcasted_iota(jnp.int32, (stack, win_keys), 0)
        ko = lax.broadcasted_iota(jnp.int32, (stack, win_keys), 1)
        absdist = jnp.abs(WINDOW + (row % CHUNK) - ko).astype(F32)
        row1 = lax.broadcasted_iota(jnp.int32, (stack, 1), 0)
        for kh in range(N_KV):
            slope = jnp.zeros((stack, win_keys), F32)
            sink = jnp.zeros((stack, 1), F32)
            for g in range(GROUP):
                h = kh * GROUP + g
                slope = jnp.where(row // CHUNK == g, 2.0 ** (-8.0 * (h + 1) / N_HEADS), slope)
                sink = jnp.where(row1 // CHUNK == g, sink_ref[h], sink)
            bias_buf[kh] = -slope * absdist
            sink_buf[kh] = sink

    x = x_ref[...]
    xb = x.astype(BF16)
    q_buf[...] = (_dot(xb, wqkv_ref[:, 0:d_q]) * (HEAD_DIM ** -0.5)).astype(BF16)
    k = _dot(xb, wqkv_ref[:, d_q:d_q + d_kv])
    v = _dot(xb, wqkv_ref[:, d_q + d_kv:d_q + 2 * d_kv])
    k_buf[WINDOW:WINDOW + tt, :] = k.astype(BF16)
    v_buf[WINDOW:WINDOW + tt, :] = v.astype(BF16)
    if tt >= WINDOW:
        k_out_ref[...] = k[tt - WINDOW:tt, :]
        v_out_ref[...] = v[tt - WINDOW:tt, :]
    else:
        k_out_ref[0:WINDOW - tt, :] = k0_ref[tt:WINDOW, :]
        v_out_ref[0:WINDOW - tt, :] = v0_ref[tt:WINDOW, :]
        k_out_ref[WINDOW - tt:WINDOW, :] = k
        v_out_ref[WINDOW - tt:WINDOW, :] = v

    def attend(r0, mask_keys):
        scores = []
        for kh in range(N_KV):
            q4 = q_buf[pl.ds(r0, CHUNK), kh * stack:(kh + 1) * stack]
            qs = jnp.concatenate([q4[:, g * HEAD_DIM:(g + 1) * HEAD_DIM] for g in range(GROUP)], axis=0)
            kwh = k_buf[pl.ds(r0, win_keys), kh * HEAD_DIM:(kh + 1) * HEAD_DIM]
            s = lax.dot_general(qs, kwh, (((1,), (1,)), ((), ())), preferred_element_type=F32)
            s = s + bias_buf[kh]
            if mask_keys:
                ko = lax.broadcasted_iota(jnp.int32, (stack, win_keys), 1)
                s = jnp.where((t * tt + r0 - WINDOW + ko) >= 0, s, NEG_INF)
            scores.append(s)
        probs, denoms = [], []
        for kh in range(N_KV):
            s = scores[kh]
            sink = sink_buf[kh]
            m = jnp.maximum(jnp.max(s, axis=-1, keepdims=True), sink)
            p = jnp.exp(s - m)
            denoms.append(jnp.sum(p, axis=-1, keepdims=True) + jnp.exp(sink - m))
            probs.append(p.astype(BF16))
        for kh in range(N_KV):
            vwh = v_buf[pl.ds(r0, win_keys), kh * HEAD_DIM:(kh + 1) * HEAD_DIM]
            o = (_dot(probs[kh], vwh) / denoms[kh]).astype(BF16)
            for g in range(GROUP):
                h = kh * GROUP + g
                o_buf[pl.ds(r0, CHUNK), h * HEAD_DIM:(h + 1) * HEAD_DIM] = o[g * CHUNK:(g + 1) * CHUNK, :]

    def chunk_body(c, carry):
        r0 = pl.multiple_of(c * CHUNK, CHUNK)
        if masked:
            lax.cond(t * tt + r0 < WINDOW, lambda: attend(r0, True), lambda: attend(r0, False))
        else:
            attend(r0, False)
        return carry

    lax.fori_loop(0, tt // CHUNK, chunk_body, 0)

    out = _dot(o_buf[...], w_out_ref[...])
    y_ref[...] = _layer_norm(alpha * x + out, lng_ref[...], lnb_ref[...])
    k_buf[0:WINDOW, :] = k_buf[tt:tt + WINDOW, :]
    v_buf[0:WINDOW, :] = v_buf[tt:tt + WINDOW, :]


def _mixer_c_call(x, k0, v0, wqkv, sinks, w_out, ln_g, ln_b, alpha, tt, masked):
    bsz, seq, d_model = x.shape
    d_q = N_HEADS * HEAD_DIM
    d_kv = N_KV * HEAD_DIM
    assert seq % tt == 0 and tt % CHUNK == 0
    assert tt >= WINDOW or seq == tt
    grid = (bsz, seq // tt)
    consts = (wqkv, w_out, ln_g, ln_b)
    in_specs = [
        pl.BlockSpec(memory_space=pltpu.SMEM),
        pl.BlockSpec((None, tt, d_model), lambda b, t: (b, t, 0)),
        pl.BlockSpec((None, WINDOW, d_kv), lambda b, t: (b, 0, 0)),
        pl.BlockSpec((None, WINDOW, d_kv), lambda b, t: (b, 0, 0)),
    ] + [_const_spec(c) for c in consts]
    out_shape = (
        jax.ShapeDtypeStruct((bsz, seq, d_model), F32),
        jax.ShapeDtypeStruct((bsz, WINDOW, d_kv), F32),
        jax.ShapeDtypeStruct((bsz, WINDOW, d_kv), F32),
    )
    out_specs = (
        pl.BlockSpec((None, tt, d_model), lambda b, t: (b, t, 0)),
        pl.BlockSpec((None, WINDOW, d_kv), lambda b, t: (b, 0, 0)),
        pl.BlockSpec((None, WINDOW, d_kv), lambda b, t: (b, 0, 0)),
    )
    scratch = [
        pltpu.VMEM((tt, d_q), BF16),
        pltpu.VMEM((tt + WINDOW, d_kv), BF16),
        pltpu.VMEM((tt + WINDOW, d_kv), BF16),
        pltpu.VMEM((tt, d_q), BF16),
        pltpu.VMEM((N_KV, GROUP * CHUNK, WINDOW + CHUNK), F32),
        pltpu.VMEM((N_KV, GROUP * CHUNK, 1), F32),
    ]
    return pl.pallas_call(
        functools.partial(_mixer_c_body, alpha, masked),
        grid=grid, in_specs=in_specs, out_specs=out_specs, out_shape=out_shape,
        scratch_shapes=scratch,
        compiler_params=pltpu.CompilerParams(
            dimension_semantics=("arbitrary", "arbitrary"), vmem_limit_bytes=VMEM_LIMIT_BYTES),
        name="mixer_c",
    )(sinks, x, k0, v0, *consts)


QBLK = 2 * CHUNK
KBLK = WINDOW + QBLK
ONES_ROWS = 16
ATTN_SKEW = 3


def _mixer_c_prompt_body(alpha, sink_ref, x_ref, wqT_ref, wk_ref, wv_ref, wvT_ref, w_out_ref, lng_ref, lnb_ref,
                         y_ref, k_out_ref, v_out_ref,
                         qT_buf, k_buf, vT_buf, oT_buf, bias_buf, sink_buf):
    tt = x_ref.shape[0]
    t = pl.program_id(1)
    n_pairs = N_HEADS // 2
    lanes = 2 * QBLK

    @pl.when(t == 0)
    def _():
        vT_buf[...] = jnp.ones(vT_buf.shape, BF16)
        k_buf[:, 0:WINDOW, :] = jnp.zeros((N_KV, WINDOW, HEAD_DIM), BF16)
        kr = lax.broadcasted_iota(jnp.int32, (KBLK, lanes), 0)
        ln = lax.broadcasted_iota(jnp.int32, (KBLK, lanes), 1)
        ql = ln % QBLK
        absdist = jnp.abs(ql + WINDOW - kr).astype(F32)
        kc, qc = kr // CHUNK, ql // CHUNK
        in_window = (kc >= qc) & (kc <= qc + WINDOW // CHUNK)
        ln1 = lax.broadcasted_iota(jnp.int32, (SUBLANES, lanes), 1)
        for pr in range(n_pairs):
            s0 = 2.0 ** (-8.0 * (2 * pr + 1) / N_HEADS)
            s1 = 2.0 ** (-8.0 * (2 * pr + 2) / N_HEADS)
            slope = jnp.where(ln < QBLK, s0, s1)
            bias_buf[pr] = jnp.where(in_window, -slope * absdist, NEG_INF)
            sink_buf[pr] = jnp.where(ln1 < QBLK, sink_ref[2 * pr], sink_ref[2 * pr + 1])

    x = x_ref[...]
    xb = x.astype(BF16)
    nt_dims = (((1,), (1,)), ((), ()))
    qT = lax.dot_general(wqT_ref[...], xb, nt_dims, preferred_element_type=F32)
    qT_buf[...] = (qT * (HEAD_DIM ** -0.5)).astype(BF16)
    k = _dot(xb, wk_ref[...])
    kb = k.astype(BF16)
    vT = lax.dot_general(wvT_ref[...], xb, nt_dims, preferred_element_type=F32)
    for kh in range(N_KV):
        k_buf[kh, WINDOW:WINDOW + tt, :] = kb[:, kh * HEAD_DIM:(kh + 1) * HEAD_DIM]
        vT_buf[kh, 0:HEAD_DIM, WINDOW:WINDOW + tt] = vT[kh * HEAD_DIM:(kh + 1) * HEAD_DIM, :].astype(BF16)
    k_out_ref[...] = k[tt - WINDOW:tt, :]

    @pl.when(t == pl.num_programs(1) - 1)
    def _():
        v_out_ref[...] = _dot(xb[tt - WINDOW:tt, :], wv_ref[...])

    def scores(qb, pr, key_lo):
        kh = pr // (GROUP // 2)
        q_lanes = slice(qb * QBLK, (qb + 1) * QBLK)
        h_rows = [slice((2 * pr + i) * HEAD_DIM, (2 * pr + i + 1) * HEAD_DIM) for i in range(2)]
        q2 = jnp.concatenate([qT_buf[h_rows[0], q_lanes], qT_buf[h_rows[1], q_lanes]], axis=1)
        keys = slice(qb * QBLK + key_lo, qb * QBLK + KBLK)
        return _dot(k_buf[kh, keys, :], q2) + bias_buf[pr, key_lo:KBLK, :]

    def finish(qb, pr, key_lo, sT):
        kh = pr // (GROUP // 2)
        q_lanes = slice(qb * QBLK, (qb + 1) * QBLK)
        h_rows = [slice((2 * pr + i) * HEAD_DIM, (2 * pr + i + 1) * HEAD_DIM) for i in range(2)]
        keys = slice(qb * QBLK + key_lo, qb * QBLK + KBLK)
        sink = sink_buf[pr][0:1, :]
        m = jnp.maximum(jnp.max(sT, axis=0, keepdims=True), sink)
        p = jnp.exp(sT - m)
        pv = _dot(vT_buf[kh, :, keys], p.astype(BF16))
        den = pv[HEAD_DIM:HEAD_DIM + 1, :] + jnp.exp(sink - m)
        oT = (pv[0:HEAD_DIM, :] * (1.0 / den)).astype(BF16)
        oT_buf[h_rows[0], q_lanes] = oT[:, 0:QBLK]
        oT_buf[h_rows[1], q_lanes] = oT[:, QBLK:lanes]

    def attend_all(first_key_lo):
        items = [(qb, pr, first_key_lo if qb == 0 else 0) for qb in range(tt // QBLK) for pr in range(n_pairs)]
        pending = []
        for item in items:
            pending.append((item, scores(*item)))
            if len(pending) > ATTN_SKEW:
                done, sT = pending.pop(0)
                finish(*done, sT)
        for done, sT in pending:
            finish(*done, sT)

    @pl.when(t == 0)
    def _():
        attend_all(WINDOW)

    @pl.when(t != 0)
    def _():
        attend_all(0)

    out = lax.dot_general(oT_buf[...], w_out_ref[...], (((0,), (0,)), ((), ())), preferred_element_type=F32)
    y_ref[...] = _layer_norm(alpha * x + out, lng_ref[...], lnb_ref[...])
    k_buf[:, 0:WINDOW, :] = k_buf[:, tt:tt + WINDOW, :]
    vT_buf[:, 0:HEAD_DIM, 0:WINDOW] = vT_buf[:, 0:HEAD_DIM, tt:tt + WINDOW]


def _mixer_c_prompt_call(x, wqT, wk, wv, wvT, sinks, w_out, ln_g, ln_b, alpha, tt):
    bsz, seq, d_model = x.shape
    d_q = N_HEADS * HEAD_DIM
    d_kv = N_KV * HEAD_DIM
    assert seq % tt == 0 and tt % QBLK == 0 and tt >= WINDOW
    consts = (wqT, wk, wv, wvT, w_out, ln_g, ln_b)
    in_specs = [
        pl.BlockSpec(memory_space=pltpu.SMEM),
        pl.BlockSpec((None, tt, d_model), lambda b, t: (b, t, 0)),
    ] + [_const_spec(c) for c in consts]
    out_shape = (
        jax.ShapeDtypeStruct((bsz, seq, d_model), F32),
        jax.ShapeDtypeStruct((bsz, WINDOW, d_kv), F32),
        jax.ShapeDtypeStruct((bsz, WINDOW, d_kv), F32),
    )
    out_specs = (
        pl.BlockSpec((None, tt, d_model), lambda b, t: (b, t, 0)),
        pl.BlockSpec((None, WINDOW, d_kv), lambda b, t: (b, 0, 0)),
        pl.BlockSpec((None, WINDOW, d_kv), lambda b, t: (b, 0, 0)),
    )
    scratch = [
        pltpu.VMEM((d_q, tt), BF16),
        pltpu.VMEM((N_KV, tt + WINDOW, HEAD_DIM), BF16),
        pltpu.VMEM((N_KV, HEAD_DIM + ONES_ROWS, tt + WINDOW), BF16),
        pltpu.VMEM((d_q, tt), BF16),
        pltpu.VMEM((N_HEADS // 2, KBLK, 2 * QBLK), F32),
        pltpu.VMEM((N_HEADS // 2, SUBLANES, 2 * QBLK), F32),
    ]
    return pl.pallas_call(
        functools.partial(_mixer_c_prompt_body, alpha),
        grid=(bsz, seq // tt), in_specs=in_specs, out_specs=out_specs, out_shape=out_shape,
        scratch_shapes=scratch,
        compiler_params=pltpu.CompilerParams(
            dimension_semantics=("arbitrary", "arbitrary"), vmem_limit_bytes=VMEM_LIMIT_BYTES),
        name="mixer_c_prompt",
    )(sinks, x, *consts)


FFN_UNROLL = 5


def _ffn_body(alpha, x_ref, wg_ref, wu_ref, wd_ref, lng_ref, lnb_ref, y_ref, xb_buf, acc_buf):
    n_chunks = wg_ref.shape[0]
    xb_buf[...] = x_ref[...].astype(BF16)

    def down(c):
        xb = xb_buf[...]
        g = _dot(xb, wg_ref[c])
        u = _dot(xb, wu_ref[c])
        hid = (g * _sigmoid(g) * u).astype(BF16)
        return _dot(hid, wd_ref[c])

    acc_buf[...] = alpha * x_ref[...] + down(0)

    def chunk_body(c, carry):
        acc_buf[...] += down(c)
        return carry

    lax.fori_loop(1, n_chunks, chunk_body, 0, unroll=FFN_UNROLL)
    y_ref[...] = _layer_norm(acc_buf[...], lng_ref[...], lnb_ref[...])


def _ffn_call(x2d, layer, wg, wu, wd, ln_g, ln_b, alpha, tm):
    rows, d_model = x2d.shape
    assert rows % tm == 0
    consts = (wg, wu, wd, ln_g, ln_b)

    def const_spec(arr):
        zeros = (0,) * (arr.ndim - 1)
        return pl.BlockSpec((None,) + arr.shape[1:], lambda i: (layer,) + zeros, pipeline_mode=pl.Buffered(1))

    return pl.pallas_call(
        functools.partial(_ffn_body, alpha),
        grid=(rows // tm,),
        in_specs=[pl.BlockSpec((tm, d_model), lambda i: (i, 0))] + [const_spec(c) for c in consts],
        out_specs=pl.BlockSpec((tm, d_model), lambda i: (i, 0)),
        out_shape=jax.ShapeDtypeStruct((rows, d_model), F32),
        scratch_shapes=[pltpu.VMEM((tm, d_model), BF16), pltpu.VMEM((tm, d_model), F32)],
        compiler_params=pltpu.CompilerParams(
            dimension_semantics=("arbitrary",), vmem_limit_bytes=VMEM_LIMIT_BYTES),
        name="ffn",
    )(x2d, *consts)


def _ffn(x, layer, ffn_prm, alpha, tm):
    bsz, seq, d_model = x.shape
    return _ffn_call(x.reshape(bsz * seq, d_model), layer, *ffn_prm, alpha, tm).reshape(x.shape)


def _row(v):
    return v.reshape(1, -1).astype(F32)


def _gate_tiles(ga_w, gx_w):
    heads, blk, _ = ga_w.shape
    per_tile = MXU_DIM // blk
    tiles = []
    for j in range(heads // per_tile):
        def diag(w):
            return jax.scipy.linalg.block_diag(*[w[j * per_tile + i] for i in range(per_tile)])
        tiles.append(jnp.concatenate([diag(ga_w), diag(gx_w)], axis=1))
    return jnp.stack(tiles).astype(BF16)


def _ffn_chunks(wg, wu, wd):
    depth, d_model, d_ff = wg.shape
    n = d_ff // MXU_DIM
    wg_c = wg.astype(BF16).reshape(depth, d_model, n, MXU_DIM).transpose(0, 2, 1, 3)
    wu_c = wu.astype(BF16).reshape(depth, d_model, n, MXU_DIM).transpose(0, 2, 1, 3)
    wd_c = wd.astype(BF16).reshape(depth, n, MXU_DIM, d_model)
    return wg_c, wu_c, wd_c


def kernel(x_prompt, x_sample, state_rec_h, state_rec_conv, state_cf_conv, cache_k, cache_v, w_in_ab, rec_conv_w, rec_conv_b, rec_gate_a_w, rec_gate_a_b, rec_gate_x_w, rec_gate_x_b, rec_lambda, cf_conv_w, cf_conv_b, cf_norm_g, cf_norm_b, w_out_ab, w_qkv, attn_sinks, w_out_c, ln_mix_g, ln_mix_b, w_ff_gate, w_ff_up, w_ff_down, ln_ff_g, ln_ff_b):
    depth = ln_mix_g.shape[0]
    alpha = (2 * depth) ** 0.25
    bp, seq_p, d_model = x_prompt.shape
    bs, seq_s, _ = x_sample.shape
    d_rec = rec_conv_w.shape[-1]
    d_conv = cf_conv_w.shape[-1]
    d_kv = N_KV * HEAD_DIM
    tt_ab = min(seq_p, 512)
    tt_c = min(seq_p, 512)
    tt_s = seq_s
    tm_p = min(bp * seq_p, 1024)
    tm_s = bs * seq_s

    ffn_prm = _ffn_chunks(w_ff_gate, w_ff_up, w_ff_down) + (
        ln_ff_g.reshape(depth, 1, d_model).astype(F32), ln_ff_b.reshape(depth, 1, d_model).astype(F32))

    yp, ys = x_prompt, x_sample
    p_h, s_h, p_rc, s_rc, p_cf, s_cf, p_k, s_k, p_v, s_v = ([] for _ in range(10))
    for layer in range(depth):
        j = layer // 2
        ln_g, ln_b = _row(ln_mix_g[layer]), _row(ln_mix_b[layer])
        if layer % 2 == 0:
            prm = (w_in_ab[j].astype(BF16), rec_conv_w[j], _row(rec_conv_b[j]),
                   _gate_tiles(rec_gate_a_w[j], rec_gate_x_w[j]), _row(rec_gate_a_b[j]), _row(rec_gate_x_b[j]),
                   _row(rec_lambda[j]), cf_conv_w[j], _row(cf_conv_b[j]), _row(cf_norm_g[j]), _row(cf_norm_b[j]),
                   w_out_ab[j].astype(BF16))
            yp, rc, hl, cf = _mixer_ab_call(
                yp, jnp.zeros((bp, REC_CONV - 1, d_rec), F32), jnp.zeros((bp, d_rec), F32),
                jnp.zeros((bp, CONV_WIDTH - 1, d_conv), F32), prm, ln_g, ln_b, alpha, tt_ab)
            p_rc.append(rc), p_h.append(hl), p_cf.append(cf)
            ys, rc, hl, cf = _mixer_ab_call(
                ys, state_rec_conv[j], state_rec_h[j], state_cf_conv[j], prm, ln_g, ln_b, alpha, tt_s)
            s_rc.append(rc), s_h.append(hl), s_cf.append(cf)
        else:
            wqkv, w_out = w_qkv[j].astype(BF16), w_out_c[j].astype(BF16)
            sinks = attn_sinks[j].astype(F32)
            d_q = N_HEADS * HEAD_DIM
            wq_t = wqkv[:, 0:d_q].T
            wk, wv = wqkv[:, d_q:d_q + d_kv], wqkv[:, d_q + d_kv:d_q + 2 * d_kv]
            yp, kk, vv = _mixer_c_prompt_call(yp, wq_t, wk, wv, wv.T, sinks, w_out, ln_g, ln_b, alpha, tt_c)
            p_k.append(kk.reshape(bp, WINDOW, N_KV, HEAD_DIM)), p_v.append(vv.reshape(bp, WINDOW, N_KV, HEAD_DIM))
            ck = cache_k[j].reshape(bs, WINDOW, d_kv)
            cv = cache_v[j].reshape(bs, WINDOW, d_kv)
            ys, kk, vv = _mixer_c_call(ys, ck, cv, wqkv, sinks, w_out, ln_g, ln_b, alpha, tt_s, False)
            s_k.append(kk.reshape(bs, WINDOW, N_KV, HEAD_DIM)), s_v.append(vv.reshape(bs, WINDOW, N_KV, HEAD_DIM))
        yp = _ffn(yp, layer, ffn_prm, alpha, tm_p)
        ys = _ffn(ys, layer, ffn_prm, alpha, tm_s)
    return (yp, ys, jnp.stack(p_h), jnp.stack(s_h), jnp.stack(p_rc), jnp.stack(s_rc),
            jnp.stack(p_cf), jnp.stack(s_cf), jnp.stack(p_k), jnp.stack(s_k), jnp.stack(p_v), jnp.stack(s_v))
```

```python
import functools

import jax
import jax.numpy as jnp
from jax import lax
from jax.experimental import pallas as pl
from jax.experimental.pallas import tpu as pltpu

CHUNK = 64
WINDOW = 128
HEAD_DIM = 64
N_HEADS = 16
N_KV = 4
GROUP = N_HEADS // N_KV
REC_CONV = 4
CONV_WIDTH = 31
LRU_C = 8.0
LN_EPS = 1e-5
NEG_INF = -1e30

SUBLANES = 8
MXU_DIM = 256
CONV_ROWS = 32
VMEM_LIMIT_BYTES = 56 * 1024 * 1024

BF16 = jnp.bfloat16
F32 = jnp.float32


def _dot(a, b):
    return jnp.dot(a, b, preferred_element_type=F32)


def _layer_norm(z, g, b):
    mu = jnp.mean(z, axis=-1, keepdims=True)
    zc = z - mu
    var = jnp.mean(zc * zc, axis=-1, keepdims=True)
    return zc * lax.rsqrt(var + LN_EPS) * g + b


def _sigmoid(x):
    return jax.nn.sigmoid(x)


def _mixer_ab_body(alpha, x_ref, rc0_ref, h0_ref, cf0_ref, w_in_ref, rcw_ref, rcb_ref, wgate_ref,
                   gab_ref, gxb_ref, lam_ref, cfw_ref, cfb_ref, cfg_ref, cfbeta_ref, w_out_ref,
                   lng_ref, lnb_ref,
                   y_ref, rc_out_ref, h_out_ref, cf_out_ref,
                   xr_buf, g_buf, a_buf, b_buf, h_buf, hcar, c_buf, wb_buf):
    bsz, tt, d_model = x_ref.shape
    rows = bsz * tt
    d_rec = rcw_ref.shape[1]
    d_conv = cfw_ref.shape[1]
    rc_rows = (REC_CONV - 1) * bsz
    cf_rows = (CONV_WIDTH - 1) * bsz
    step = pl.program_id(0)

    @pl.when(step == 0)
    def _():
        xr_buf[0:rc_rows, :] = rc0_ref[...].reshape(rc_rows, d_rec)
        g_buf[0:cf_rows, :] = cf0_ref[...].reshape(cf_rows, d_conv)
        hcar[...] = h0_ref[...]
        for k in range(CONV_WIDTH):
            wb_buf[k] = jnp.broadcast_to(cfw_ref[k:k + 1, :], (SUBLANES, d_conv))

    x = pltpu.einshape("btd->tbd", x_ref[...]).reshape(rows, d_model)
    xb = x.astype(BF16)

    xr = _dot(xb, w_in_ref[:, 0:d_rec])
    yr = _dot(xb, w_in_ref[:, d_rec:2 * d_rec])
    cv = _dot(xb, w_in_ref[:, 2 * d_rec:2 * d_rec + d_conv])
    cg = _dot(xb, w_in_ref[:, 2 * d_rec + d_conv:2 * d_rec + 2 * d_conv])

    xr_buf[rc_rows:rc_rows + rows, :] = xr
    xc = rcb_ref[...] + rcw_ref[REC_CONV - 1:REC_CONV, :] * xr
    for k in range(REC_CONV - 1):
        xc = xc + rcw_ref[k:k + 1, :] * xr_buf[k * bsz:k * bsz + rows, :]
    xcb = xc.astype(BF16)
    r_parts, i_parts = [], []
    for j in range(d_rec // MXU_DIM):
        lo, hi = j * MXU_DIM, (j + 1) * MXU_DIM
        gj = _dot(xcb[:, lo:hi], wgate_ref[j])
        r_parts.append(_sigmoid(gj[:, :MXU_DIM] + gab_ref[:, lo:hi]))
        i_parts.append(_sigmoid(gj[:, MXU_DIM:] + gxb_ref[:, lo:hi]))
    r = jnp.concatenate(r_parts, axis=1)
    i = jnp.concatenate(i_parts, axis=1)
    nlam = -lam_ref[...]
    softplus = jnp.maximum(nlam, 0.0) + jnp.log1p(jnp.exp(-jnp.abs(nlam)))
    log_a = (-LRU_C * softplus) * r
    a = jnp.exp(log_a)
    y = 1.0 - a * a
    a_buf[...] = a
    b_buf[...] = jnp.where(y > 0.0, y * lax.rsqrt(y), 0.0) * (i * xc)

    h = hcar[...]
    for t in range(tt):
        grp = slice(t * bsz, (t + 1) * bsz)
        h = a_buf[grp, :] * h + b_buf[grp, :]
        h_buf[grp, :] = h
    hcar[...] = h

    rec_out = h_buf[...] * jax.nn.gelu(yr)

    g_buf[cf_rows:cf_rows + rows, :] = cv * _sigmoid(cg)
    blk_groups = CONV_ROWS // SUBLANES

    def conv_block(rb, carry):
        row0 = pl.multiple_of(rb * CONV_ROWS, CONV_ROWS)
        acc = jnp.broadcast_to(cfb_ref[...], (CONV_ROWS, d_conv)).reshape(blk_groups, SUBLANES, d_conv)
        for k in range(CONV_WIDTH):
            src = g_buf[pl.ds(row0 + k * bsz, CONV_ROWS), :]
            acc = acc + wb_buf[k] * src.reshape(blk_groups, SUBLANES, d_conv)
        c_buf[pl.ds(row0, CONV_ROWS), :] = acc.reshape(CONV_ROWS, d_conv)
        return carry

    lax.fori_loop(0, rows // CONV_ROWS, conv_block, 0, unroll=True)
    cn = _layer_norm(c_buf[...], cfg_ref[...], cfbeta_ref[...])
    c2 = cn * _sigmoid(cn)

    out = _dot(rec_out.astype(BF16), w_out_ref[0:d_rec, :]) + _dot(c2.astype(BF16), w_out_ref[d_rec:d_rec + d_conv, :])
    y_ref[...] = _layer_norm(alpha * x + out, lng_ref[...], lnb_ref[...]).reshape(tt, bsz, d_model)

    new_rc = xr_buf[rows:rows + rc_rows, :]
    new_cf = g_buf[rows:rows + cf_rows, :]
    xr_buf[0:rc_rows, :] = new_rc
    g_buf[0:cf_rows, :] = new_cf
    rc_out_ref[...] = new_rc.reshape(REC_CONV - 1, bsz, d_rec)
    cf_out_ref[...] = new_cf.reshape(CONV_WIDTH - 1, bsz, d_conv)
    h_out_ref[...] = h


def _const_spec(arr):
    zeros = (0,) * arr.ndim
    return pl.BlockSpec(arr.shape, lambda b, t: zeros)


def _mixer_ab_call(x, rc0, h0, cf0, prm, ln_g, ln_b, alpha, tt):
    bsz, seq, d_model = x.shape
    (w_in, rcw, rcb, wgate, gab, gxb, lam, cfw, cfb, cfg, cfbeta, w_out) = prm
    d_rec, d_conv = rcw.shape[1], cfw.shape[1]
    assert bsz == SUBLANES and seq % tt == 0 and (bsz * tt) % CONV_ROWS == 0
    consts = (w_in, rcw, rcb, wgate, gab, gxb, lam, cfw, cfb, cfg, cfbeta, w_out, ln_g, ln_b)
    states = (jnp.swapaxes(rc0, 0, 1), h0, jnp.swapaxes(cf0, 0, 1))

    def whole(arr):
        zeros = (0,) * arr.ndim
        return pl.BlockSpec(arr.shape, lambda s: zeros)

    state_shapes = (
        jax.ShapeDtypeStruct((REC_CONV - 1, bsz, d_rec), F32),
        jax.ShapeDtypeStruct((bsz, d_rec), F32),
        jax.ShapeDtypeStruct((CONV_WIDTH - 1, bsz, d_conv), F32),
    )
    rows = bsz * tt
    scratch = [
        pltpu.VMEM(((REC_CONV - 1) * bsz + rows, d_rec), F32),
        pltpu.VMEM(((CONV_WIDTH - 1) * bsz + rows, d_conv), F32),
        pltpu.VMEM((rows, d_rec), F32),
        pltpu.VMEM((rows, d_rec), F32),
        pltpu.VMEM((rows, d_rec), F32),
        pltpu.VMEM((bsz, d_rec), F32),
        pltpu.VMEM((rows, d_conv), F32),
        pltpu.VMEM((CONV_WIDTH, SUBLANES, d_conv), F32),
    ]
    y, rc, h, cf = pl.pallas_call(
        functools.partial(_mixer_ab_body, alpha),
        grid=(seq // tt,),
        in_specs=[pl.BlockSpec((bsz, tt, d_model), lambda s: (0, s, 0))]
        + [whole(a) for a in states] + [whole(c) for c in consts],
        out_specs=(pl.BlockSpec((tt, bsz, d_model), lambda s: (s, 0, 0)),) + tuple(whole(a) for a in state_shapes),
        out_shape=(jax.ShapeDtypeStruct((seq, bsz, d_model), F32),) + state_shapes,
        scratch_shapes=scratch,
        compiler_params=pltpu.CompilerParams(
            dimension_semantics=("arbitrary",), vmem_limit_bytes=VMEM_LIMIT_BYTES),
        name="mixer_ab",
    )(x, *states, *consts)
    return y, jnp.swapaxes(rc, 0, 1), h, jnp.swapaxes(cf, 0, 1)


def _x_operand(x, tt, time_major):
    d_model = x.shape[-1]
    if time_major:
        return x.reshape(x.shape[0], -1), pl.BlockSpec((tt, d_model), lambda b, t: (t, b))
    return x, pl.BlockSpec((None, tt, d_model), lambda b, t: (b, t, 0))


def _mixer_c_body(alpha, sink_ref, x_ref, k0_ref, v0_ref, wqkv_ref, w_out_ref, lng_ref, lnb_ref,
                  y_ref, k_out_ref, v_out_ref,
                  q_buf, k_buf, v_buf, o_buf, bias_buf, sink_buf):
    tt = x_ref.shape[0]
    d_q = N_HEADS * HEAD_DIM
    d_kv = N_KV * HEAD_DIM
    win_keys = WINDOW + CHUNK
    stack = GROUP * CHUNK
    t = pl.program_id(1)

    @pl.when(t == 0)
    def _():
        k_buf[0:WINDOW, :] = k0_ref[...].astype(BF16)
        v_buf[0:WINDOW, :] = v0_ref[...].astype(BF16)
        row = lax.broadcasted_iota(jnp.int32, (stack, win_keys), 0)
        ko = lax.broadcasted_iota(jnp.int32, (stack, win_keys), 1)
        absdist = jnp.abs(WINDOW + (row % CHUNK) - ko).astype(F32)
        row1 = lax.broadcasted_iota(jnp.int32, (stack, 1), 0)
        for kh in range(N_KV):
            slope = jnp.zeros((stack, win_keys), F32)
            sink = jnp.zeros((stack, 1), F32)
            for g in range(GROUP):
                h = kh * GROUP + g
                slope = jnp.where(row // CHUNK == g, 2.0 ** (-8.0 * (h + 1) / N_HEADS), slope)
                sink = jnp.where(row1 // CHUNK == g, sink_ref[h], sink)
            bias_buf[kh] = -slope * absdist
            sink_buf[kh] = sink

    x = x_ref[...]
    xb = x.astype(BF16)
    q_buf[...] = (_dot(xb, wqkv_ref[:, 0:d_q]) * (HEAD_DIM ** -0.5)).astype(BF16)
    k = _dot(xb, wqkv_ref[:, d_q:d_q + d_kv])
    v = _dot(xb, wqkv_ref[:, d_q + d_kv:d_q + 2 * d_kv])
    k_buf[WINDOW:WINDOW + tt, :] = k.astype(BF16)
    v_buf[WINDOW:WINDOW + tt, :] = v.astype(BF16)
    if tt >= WINDOW:
        k_out_ref[...] = k[tt - WINDOW:tt, :]
        v_out_ref[...] = v[tt - WINDOW:tt, :]
    else:
        k_out_ref[0:WINDOW - tt, :] = k0_ref[tt:WINDOW, :]
        v_out_ref[0:WINDOW - tt, :] = v0_ref[tt:WINDOW, :]
        k_out_ref[WINDOW - tt:WINDOW, :] = k
        v_out_ref[WINDOW - tt:WINDOW, :] = v

    def chunk_body(c, carry):
        r0 = pl.multiple_of(c * CHUNK, CHUNK)
        scores = []
        for kh in range(N_KV):
            q4 = q_buf[pl.ds(r0, CHUNK), kh * stack:(kh + 1) * stack]
            qs = jnp.concatenate([q4[:, g * HEAD_DIM:(g + 1) * HEAD_DIM] for g in range(GROUP)], axis=0)
            kwh = k_buf[pl.ds(r0, win_keys), kh * HEAD_DIM:(kh + 1) * HEAD_DIM]
            s = lax.dot_general(qs, kwh, (((1,), (1,)), ((), ())), preferred_element_type=F32)
            scores.append(s + bias_buf[kh])
        probs, denoms = [], []
        for kh in range(N_KV):
            s = scores[kh]
            sink = sink_buf[kh]
            m = jnp.maximum(jnp.max(s, axis=-1, keepdims=True), sink)
            p = jnp.exp(s - m)
            denoms.append(jnp.sum(p, axis=-1, keepdims=True) + jnp.exp(sink - m))
            probs.append(p.astype(BF16))
        for kh in range(N_KV):
            vwh = v_buf[pl.ds(r0, win_keys), kh * HEAD_DIM:(kh + 1) * HEAD_DIM]
            o = (_dot(probs[kh], vwh) / denoms[kh]).astype(BF16)
            for g in range(GROUP):
                h = kh * GROUP + g
                o_buf[pl.ds(r0, CHUNK), h * HEAD_DIM:(h + 1) * HEAD_DIM] = o[g * CHUNK:(g + 1) * CHUNK, :]
        return carry

    lax.fori_loop(0, tt // CHUNK, chunk_body, 0)

    out = _dot(o_buf[...], w_out_ref[...])
    y_ref[...] = _layer_norm(alpha * x + out, lng_ref[...], lnb_ref[...])
    k_buf[0:WINDOW, :] = k_buf[tt:tt + WINDOW, :]
    v_buf[0:WINDOW, :] = v_buf[tt:tt + WINDOW, :]


def _mixer_c_call(x, time_major, k0, v0, wqkv, sinks, w_out, ln_g, ln_b, alpha, tt):
    seq, bsz = (x.shape[0], x.shape[1]) if time_major else (x.shape[1], x.shape[0])
    d_model = x.shape[-1]
    d_q = N_HEADS * HEAD_DIM
    d_kv = N_KV * HEAD_DIM
    assert seq % tt == 0 and tt % CHUNK == 0
    assert tt >= WINDOW or seq == tt
    grid = (bsz, seq // tt)
    consts = (wqkv, w_out, ln_g, ln_b)
    x, x_spec = _x_operand(x, tt, time_major)
    in_specs = [
        pl.BlockSpec(memory_space=pltpu.SMEM),
        x_spec,
        pl.BlockSpec((None, WINDOW, d_kv), lambda b, t: (b, 0, 0)),
        pl.BlockSpec((None, WINDOW, d_kv), lambda b, t: (b, 0, 0)),
    ] + [_const_spec(c) for c in consts]
    out_shape = (
        jax.ShapeDtypeStruct((bsz, seq, d_model), F32),
        jax.ShapeDtypeStruct((bsz, WINDOW, d_kv), F32),
        jax.ShapeDtypeStruct((bsz, WINDOW, d_kv), F32),
    )
    out_specs = (
        pl.BlockSpec((None, tt, d_model), lambda b, t: (b, t, 0)),
        pl.BlockSpec((None, WINDOW, d_kv), lambda b, t: (b, 0, 0)),
        pl.BlockSpec((None, WINDOW, d_kv), lambda b, t: (b, 0, 0)),
    )
    scratch = [
        pltpu.VMEM((tt, d_q), BF16),
        pltpu.VMEM((tt + WINDOW, d_kv), BF16),
        pltpu.VMEM((tt + WINDOW, d_kv), BF16),
        pltpu.VMEM((tt, d_q), BF16),
        pltpu.VMEM((N_KV, GROUP * CHUNK, WINDOW + CHUNK), F32),
        pltpu.VMEM((N_KV, GROUP * CHUNK, 1), F32),
    ]
    return pl.pallas_call(
        functools.partial(_mixer_c_body, alpha),
        grid=grid, in_specs=in_specs, out_specs=out_specs, out_shape=out_shape,
        scratch_shapes=scratch,
        compiler_params=pltpu.CompilerParams(
            dimension_semantics=("arbitrary", "arbitrary"), vmem_limit_bytes=VMEM_LIMIT_BYTES),
        name="mixer_c",
    )(sinks, x, k0, v0, *consts)


QBLK = 2 * CHUNK
KBLK = WINDOW + QBLK
ONES_ROWS = 16
ATTN_SKEW = 4


def _mixer_c_prompt_body(alpha, sink_ref, x_ref, wqT_ref, wk_ref, wv_ref, wvT_ref, w_out_ref, lng_ref, lnb_ref,
                         y_ref, k_out_ref, v_out_ref,
                         qT_buf, k_buf, vT_buf, oT_buf, bias_buf, sink_buf):
    tt = x_ref.shape[0]
    t = pl.program_id(1)
    n_pairs = N_HEADS // 2
    lanes = 2 * QBLK

    @pl.when(t == 0)
    def _():
        vT_buf[...] = jnp.ones(vT_buf.shape, BF16)
        k_buf[:, 0:WINDOW, :] = jnp.zeros((N_KV, WINDOW, HEAD_DIM), BF16)
        kr = lax.broadcasted_iota(jnp.int32, (KBLK, lanes), 0)
        ln = lax.broadcasted_iota(jnp.int32, (KBLK, lanes), 1)
        ql = ln % QBLK
        absdist = jnp.abs(ql + WINDOW - kr).astype(F32)
        kc, qc = kr // CHUNK, ql // CHUNK
        in_window = (kc >= qc) & (kc <= qc + WINDOW // CHUNK)
        ln1 = lax.broadcasted_iota(jnp.int32, (SUBLANES, lanes), 1)
        for pr in range(n_pairs):
            s0 = 2.0 ** (-8.0 * (2 * pr + 1) / N_HEADS)
            s1 = 2.0 ** (-8.0 * (2 * pr + 2) / N_HEADS)
            slope = jnp.where(ln < QBLK, s0, s1)
            bias_buf[pr] = jnp.where(in_window, -slope * absdist, NEG_INF)
            sink_buf[pr] = jnp.where(ln1 < QBLK, sink_ref[2 * pr], sink_ref[2 * pr + 1])

    x = x_ref[...]
    xb = x.astype(BF16)
    nt_dims = (((1,), (1,)), ((), ()))
    qT = lax.dot_general(wqT_ref[...], xb, nt_dims, preferred_element_type=F32)
    qT_buf[...] = (qT * (HEAD_DIM ** -0.5)).astype(BF16)
    k = _dot(xb, wk_ref[...])
    kb = k.astype(BF16)
    vT = lax.dot_general(wvT_ref[...], xb, nt_dims, preferred_element_type=F32)
    for kh in range(N_KV):
        k_buf[kh, WINDOW:WINDOW + tt, :] = kb[:, kh * HEAD_DIM:(kh + 1) * HEAD_DIM]
        vT_buf[kh, 0:HEAD_DIM, WINDOW:WINDOW + tt] = vT[kh * HEAD_DIM:(kh + 1) * HEAD_DIM, :].astype(BF16)
    k_out_ref[...] = k[tt - WINDOW:tt, :]

    @pl.when(t == pl.num_programs(1) - 1)
    def _():
        v_out_ref[...] = _dot(xb[tt - WINDOW:tt, :], wv_ref[...])

    def scores(qb, pr, key_lo):
        kh = pr // (GROUP // 2)
        q_lanes = slice(qb * QBLK, (qb + 1) * QBLK)
        h_rows = [slice((2 * pr + i) * HEAD_DIM, (2 * pr + i + 1) * HEAD_DIM) for i in range(2)]
        q2 = jnp.concatenate([qT_buf[h_rows[0], q_lanes], qT_buf[h_rows[1], q_lanes]], axis=1)
        keys = slice(qb * QBLK + key_lo, qb * QBLK + KBLK)
        return _dot(k_buf[kh, keys, :], q2) + bias_buf[pr, key_lo:KBLK, :]

    def finish(qb, pr, key_lo, sT):
        kh = pr // (GROUP // 2)
        q_lanes = slice(qb * QBLK, (qb + 1) * QBLK)
        h_rows = [slice((2 * pr + i) * HEAD_DIM, (2 * pr + i + 1) * HEAD_DIM) for i in range(2)]
        keys = slice(qb * QBLK + key_lo, qb * QBLK + KBLK)
        sink = sink_buf[pr][0:1, :]
        m = jnp.maximum(jnp.max(sT, axis=0, keepdims=True), sink)
        p = jnp.exp(sT - m)
        pv = _dot(vT_buf[kh, :, keys], p.astype(BF16))
        den = pv[HEAD_DIM:HEAD_DIM + 1, :] + jnp.exp(sink - m)
        oT = (pv[0:HEAD_DIM, :] * (1.0 / den)).astype(BF16)
        oT_buf[h_rows[0], q_lanes] = oT[:, 0:QBLK]
        oT_buf[h_rows[1], q_lanes] = oT[:, QBLK:lanes]

    def attend_all(first_key_lo):
        items = [(qb, pr, first_key_lo if qb == 0 else 0) for qb in range(tt // QBLK) for pr in range(n_pairs)]
        pending = []
        for item in items:
            pending.append((item, scores(*item)))
            if len(pending) > ATTN_SKEW:
                done, sT = pending.pop(0)
                finish(*done, sT)
        for done, sT in pending:
            finish(*done, sT)

    @pl.when(t == 0)
    def _():
        attend_all(WINDOW)

    @pl.when(t != 0)
    def _():
        attend_all(0)

    out = lax.dot_general(oT_buf[...], w_out_ref[...], (((0,), (0,)), ((), ())), preferred_element_type=F32)
    y_ref[...] = _layer_norm(alpha * x + out, lng_ref[...], lnb_ref[...])
    k_buf[:, 0:WINDOW, :] = k_buf[:, tt:tt + WINDOW, :]
    vT_buf[:, 0:HEAD_DIM, 0:WINDOW] = vT_buf[:, 0:HEAD_DIM, tt:tt + WINDOW]


def _mixer_c_prompt_call(x, time_major, wqT, wk, wv, wvT, sinks, w_out, ln_g, ln_b, alpha, tt):
    seq, bsz = (x.shape[0], x.shape[1]) if time_major else (x.shape[1], x.shape[0])
    d_model = x.shape[-1]
    d_q = N_HEADS * HEAD_DIM
    d_kv = N_KV * HEAD_DIM
    assert seq % tt == 0 and tt % QBLK == 0 and tt >= WINDOW
    consts = (wqT, wk, wv, wvT, w_out, ln_g, ln_b)
    x, x_spec = _x_operand(x, tt, time_major)
    in_specs = [
        pl.BlockSpec(memory_space=pltpu.SMEM),
        x_spec,
    ] + [_const_spec(c) for c in consts]
    out_shape = (
        jax.ShapeDtypeStruct((bsz, seq, d_model), F32),
        jax.ShapeDtypeStruct((bsz, WINDOW, d_kv), F32),
        jax.ShapeDtypeStruct((bsz, WINDOW, d_kv), F32),
    )
    out_specs = (
        pl.BlockSpec((None, tt, d_model), lambda b, t: (b, t, 0)),
        pl.BlockSpec((None, WINDOW, d_kv), lambda b, t: (b, 0, 0)),
        pl.BlockSpec((None, WINDOW, d_kv), lambda b, t: (b, 0, 0)),
    )
    scratch = [
        pltpu.VMEM((d_q, tt), BF16),
        pltpu.VMEM((N_KV, tt + WINDOW, HEAD_DIM), BF16),
        pltpu.VMEM((N_KV, HEAD_DIM + ONES_ROWS, tt + WINDOW), BF16),
        pltpu.VMEM((d_q, tt), BF16),
        pltpu.VMEM((N_HEADS // 2, KBLK, 2 * QBLK), F32),
        pltpu.VMEM((N_HEADS // 2, SUBLANES, 2 * QBLK), F32),
    ]
    return pl.pallas_call(
        functools.partial(_mixer_c_prompt_body, alpha),
        grid=(bsz, seq // tt), in_specs=in_specs, out_specs=out_specs, out_shape=out_shape,
        scratch_shapes=scratch,
        compiler_params=pltpu.CompilerParams(
            dimension_semantics=("arbitrary", "arbitrary"), vmem_limit_bytes=VMEM_LIMIT_BYTES),
        name="mixer_c_prompt",
    )(sinks, x, *consts)


FFN_UNROLL = 5


def _ffn_body(alpha, x_ref, wg_ref, wu_ref, wd_ref, lng_ref, lnb_ref, y_ref, xb_buf, acc_buf):
    n_chunks = wg_ref.shape[0]
    xb_buf[...] = x_ref[...].astype(BF16)

    def down(c):
        xb = xb_buf[...]
        g = _dot(xb, wg_ref[c])
        u = _dot(xb, wu_ref[c])
        hid = (g * _sigmoid(g) * u).astype(BF16)
        return _dot(hid, wd_ref[c])

    acc_buf[...] = alpha * x_ref[...] + down(0)

    def chunk_body(c, carry):
        acc_buf[...] += down(c)
        return carry

    lax.fori_loop(1, n_chunks, chunk_body, 0, unroll=FFN_UNROLL)
    y_ref[...] = _layer_norm(acc_buf[...], lng_ref[...], lnb_ref[...])


def _ffn_call(x2d, layer, wg, wu, wd, ln_g, ln_b, alpha, tm):
    rows, d_model = x2d.shape
    assert rows % tm == 0
    consts = (wg, wu, wd, ln_g, ln_b)

    def const_spec(arr):
        zeros = (0,) * (arr.ndim - 1)
        return pl.BlockSpec((None,) + arr.shape[1:], lambda i: (layer,) + zeros, pipeline_mode=pl.Buffered(1))

    return pl.pallas_call(
        functools.partial(_ffn_body, alpha),
        grid=(rows // tm,),
        in_specs=[pl.BlockSpec((tm, d_model), lambda i: (i, 0))] + [const_spec(c) for c in consts],
        out_specs=pl.BlockSpec((tm, d_model), lambda i: (i, 0)),
        out_shape=jax.ShapeDtypeStruct((rows, d_model), F32),
        scratch_shapes=[pltpu.VMEM((tm, d_model), BF16), pltpu.VMEM((tm, d_model), F32)],
        compiler_params=pltpu.CompilerParams(
            dimension_semantics=("arbitrary",), vmem_limit_bytes=VMEM_LIMIT_BYTES),
        name="ffn",
    )(x2d, *consts)


def _ffn(x, layer, ffn_prm, alpha, tm):
    d_model = x.shape[-1]
    return _ffn_call(x.reshape(-1, d_model), layer, *ffn_prm, alpha, tm).reshape(x.shape)


def _row(v):
    return v.reshape(1, -1).astype(F32)


def _gate_tiles(ga_w, gx_w):
    heads, blk, _ = ga_w.shape
    per_tile = MXU_DIM // blk
    tiles = []
    for j in range(heads // per_tile):
        def diag(w):
            return jax.scipy.linalg.block_diag(*[w[j * per_tile + i] for i in range(per_tile)])
        tiles.append(jnp.concatenate([diag(ga_w), diag(gx_w)], axis=1))
    return jnp.stack(tiles).astype(BF16)


def _ffn_chunks(wg, wu, wd):
    depth, d_model, d_ff = wg.shape
    n = d_ff // MXU_DIM
    wg_c = wg.astype(BF16).reshape(depth, d_model, n, MXU_DIM).transpose(0, 2, 1, 3)
    wu_c = wu.astype(BF16).reshape(depth, d_model, n, MXU_DIM).transpose(0, 2, 1, 3)
    wd_c = wd.astype(BF16).reshape(depth, n, MXU_DIM, d_model)
    return wg_c, wu_c, wd_c


def kernel(x_prompt, x_sample, state_rec_h, state_rec_conv, state_cf_conv, cache_k, cache_v, w_in_ab, rec_conv_w, rec_conv_b, rec_gate_a_w, rec_gate_a_b, rec_gate_x_w, rec_gate_x_b, rec_lambda, cf_conv_w, cf_conv_b, cf_norm_g, cf_norm_b, w_out_ab, w_qkv, attn_sinks, w_out_c, ln_mix_g, ln_mix_b, w_ff_gate, w_ff_up, w_ff_down, ln_ff_g, ln_ff_b):
    depth = ln_mix_g.shape[0]
    alpha = (2 * depth) ** 0.25
    bp, seq_p, d_model = x_prompt.shape
    bs, seq_s, _ = x_sample.shape
    d_rec = rec_conv_w.shape[-1]
    d_conv = cf_conv_w.shape[-1]
    d_kv = N_KV * HEAD_DIM
    tt_ab = min(seq_p, 64)
    tt_c = min(seq_p, 512)
    tt_s = seq_s
    tm_p = min(bp * seq_p, 1024)
    tm_s = bs * seq_s

    ffn_prm = _ffn_chunks(w_ff_gate, w_ff_up, w_ff_down) + (
        ln_ff_g.reshape(depth, 1, d_model).astype(F32), ln_ff_b.reshape(depth, 1, d_model).astype(F32))

    yp, ys = x_prompt, x_sample
    time_major = False
    p_h, s_h, p_rc, s_rc, p_cf, s_cf, p_k, s_k, p_v, s_v = ([] for _ in range(10))
    for layer in range(depth):
        j = layer // 2
        ln_g, ln_b = _row(ln_mix_g[layer]), _row(ln_mix_b[layer])
        if layer % 2 == 0:
            if time_major:
                yp, ys = jnp.swapaxes(yp, 0, 1), jnp.swapaxes(ys, 0, 1)
            prm = (w_in_ab[j].astype(BF16), rec_conv_w[j], _row(rec_conv_b[j]),
                   _gate_tiles(rec_gate_a_w[j], rec_gate_x_w[j]), _row(rec_gate_a_b[j]), _row(rec_gate_x_b[j]),
                   _row(rec_lambda[j]), cf_conv_w[j], _row(cf_conv_b[j]), _row(cf_norm_g[j]), _row(cf_norm_b[j]),
                   w_out_ab[j].astype(BF16))
            yp, rc, hl, cf = _mixer_ab_call(
                yp, jnp.zeros((bp, REC_CONV - 1, d_rec), F32), jnp.zeros((bp, d_rec), F32),
                jnp.zeros((bp, CONV_WIDTH - 1, d_conv), F32), prm, ln_g, ln_b, alpha, tt_ab)
            p_rc.append(rc), p_h.append(hl), p_cf.append(cf)
            ys, rc, hl, cf = _mixer_ab_call(
                ys, state_rec_conv[j], state_rec_h[j], state_cf_conv[j], prm, ln_g, ln_b, alpha, tt_s)
            s_rc.append(rc), s_h.append(hl), s_cf.append(cf)
            time_major = True
        else:
            wqkv, w_out = w_qkv[j].astype(BF16), w_out_c[j].astype(BF16)
            sinks = attn_sinks[j].astype(F32)
            d_q = N_HEADS * HEAD_DIM
            wq_t = wqkv[:, 0:d_q].T
            wk, wv = wqkv[:, d_q:d_q + d_kv], wqkv[:, d_q + d_kv:d_q + 2 * d_kv]
            yp, kk, vv = _mixer_c_prompt_call(yp, time_major, wq_t, wk, wv, wv.T, sinks, w_out, ln_g, ln_b, alpha, tt_c)
            p_k.append(kk.reshape(bp, WINDOW, N_KV, HEAD_DIM)), p_v.append(vv.reshape(bp, WINDOW, N_KV, HEAD_DIM))
            ck = cache_k[j].reshape(bs, WINDOW, d_kv)
            cv = cache_v[j].reshape(bs, WINDOW, d_kv)
            ys, kk, vv = _mixer_c_call(ys, time_major, ck, cv, wqkv, sinks, w_out, ln_g, ln_b, alpha, tt_s)
            s_k.append(kk.reshape(bs, WINDOW, N_KV, HEAD_DIM)), s_v.append(vv.reshape(bs, WINDOW, N_KV, HEAD_DIM))
            time_major = False
        yp = _ffn(yp, layer, ffn_prm, alpha, tm_p)
        ys = _ffn(ys, layer, ffn_prm, alpha, tm_s)
    if time_major:
        yp, ys = jnp.swapaxes(yp, 0, 1), jnp.swapaxes(ys, 0, 1)
    return (yp, ys, jnp.stack(p_h), jnp.stack(s_h), jnp.stack(p_rc), jnp.stack(s_rc),
            jnp.stack(p_cf), jnp.stack(s_cf), jnp.stack(p_k), jnp.stack(s_k), jnp.stack(p_v), jnp.stack(s_v))
```

```python
import functools

import jax
import jax.numpy as jnp
from jax import lax
from jax.experimental import pallas as pl
from jax.experimental.pallas import tpu as pltpu

CHUNK = 64
WINDOW = 128
HEAD_DIM = 64
N_HEADS = 16
N_KV = 4
GROUP = N_HEADS // N_KV
REC_CONV = 4
CONV_WIDTH = 31
LRU_C = 8.0
LN_EPS = 1e-5
NEG_INF = -1e30

SUBLANES = 8
MXU_DIM = 256
CONV_ROWS = 32
VMEM_LIMIT_BYTES = 56 * 1024 * 1024

BF16 = jnp.bfloat16
F32 = jnp.float32


def _dot(a, b):
    return jnp.dot(a, b, preferred_element_type=F32)


def _layer_norm(z, g, b):
    mu = jnp.mean(z, axis=-1, keepdims=True)
    zc = z - mu
    var = jnp.mean(zc * zc, axis=-1, keepdims=True)
    return zc * lax.rsqrt(var + LN_EPS) * g + b


def _sigmoid(x):
    return jax.nn.sigmoid(x)


def _mixer_ab_body(alpha, x_ref, rc0_ref, h0_ref, cf0_ref, w_in_ref, rcw_ref, rcb_ref, wgate_ref,
                   gab_ref, gxb_ref, lam_ref, cfw_ref, cfb_ref, cfg_ref, cfbeta_ref, w_out_ref,
                   lng_ref, lnb_ref,
                   y_ref, rc_out_ref, h_out_ref, cf_out_ref,
                   xr_buf, g_buf, a_buf, b_buf, h_buf, hcar, c_buf, wb_buf):
    bsz, tt, d_model = x_ref.shape
    rows = bsz * tt
    d_rec = rcw_ref.shape[1]
    d_conv = cfw_ref.shape[1]
    rc_rows = (REC_CONV - 1) * bsz
    cf_rows = (CONV_WIDTH - 1) * bsz
    step = pl.program_id(0)

    @pl.when(step == 0)
    def _():
        xr_buf[0:rc_rows, :] = rc0_ref[...].reshape(rc_rows, d_rec)
        g_buf[0:cf_rows, :] = cf0_ref[...].reshape(cf_rows, d_conv)
        hcar[...] = h0_ref[...]
        for k in range(CONV_WIDTH):
            wb_buf[k] = jnp.broadcast_to(cfw_ref[k:k + 1, :], (SUBLANES, d_conv))

    x = pltpu.einshape("btd->tbd", x_ref[...]).reshape(rows, d_model)
    xb = x.astype(BF16)

    xr = _dot(xb, w_in_ref[:, 0:d_rec])
    yr = _dot(xb, w_in_ref[:, d_rec:2 * d_rec])
    cv = _dot(xb, w_in_ref[:, 2 * d_rec:2 * d_rec + d_conv])
    cg = _dot(xb, w_in_ref[:, 2 * d_rec + d_conv:2 * d_rec + 2 * d_conv])

    xr_buf[rc_rows:rc_rows + rows, :] = xr
    xc = rcb_ref[...] + rcw_ref[REC_CONV - 1:REC_CONV, :] * xr
    for k in range(REC_CONV - 1):
        xc = xc + rcw_ref[k:k + 1, :] * xr_buf[k * bsz:k * bsz + rows, :]
    xcb = xc.astype(BF16)
    r_parts, i_parts = [], []
    for j in range(d_rec // MXU_DIM):
        lo, hi = j * MXU_DIM, (j + 1) * MXU_DIM
        gj = _dot(xcb[:, lo:hi], wgate_ref[j])
        r_parts.append(_sigmoid(gj[:, :MXU_DIM] + gab_ref[:, lo:hi]))
        i_parts.append(_sigmoid(gj[:, MXU_DIM:] + gxb_ref[:, lo:hi]))
    r = jnp.concatenate(r_parts, axis=1)
    i = jnp.concatenate(i_parts, axis=1)
    nlam = -lam_ref[...]
    softplus = jnp.maximum(nlam, 0.0) + jnp.log1p(jnp.exp(-jnp.abs(nlam)))
    log_a = (-LRU_C * softplus) * r
    a = jnp.exp(log_a)
    y = 1.0 - a * a
    a_buf[...] = a
    b_buf[...] = jnp.where(y > 0.0, y * lax.rsqrt(y), 0.0) * (i * xc)

    h = hcar[...]
    for t in range(tt):
        grp = slice(t * bsz, (t + 1) * bsz)
        h = a_buf[grp, :] * h + b_buf[grp, :]
        h_buf[grp, :] = h
    hcar[...] = h

    rec_out = h_buf[...] * jax.nn.gelu(yr)

    g_buf[cf_rows:cf_rows + rows, :] = cv * _sigmoid(cg)
    blk_groups = CONV_ROWS // SUBLANES

    def conv_block(rb, carry):
        row0 = pl.multiple_of(rb * CONV_ROWS, CONV_ROWS)
        acc = jnp.broadcast_to(cfb_ref[...], (CONV_ROWS, d_conv)).reshape(blk_groups, SUBLANES, d_conv)
        for k in range(CONV_WIDTH):
            src = g_buf[pl.ds(row0 + k * bsz, CONV_ROWS), :]
            acc = acc + wb_buf[k] * src.reshape(blk_groups, SUBLANES, d_conv)
        c_buf[pl.ds(row0, CONV_ROWS), :] = acc.reshape(CONV_ROWS, d_conv)
        return carry

    lax.fori_loop(0, rows // CONV_ROWS, conv_block, 0, unroll=True)
    cn = _layer_norm(c_buf[...], cfg_ref[...], cfbeta_ref[...])
    c2 = cn * _sigmoid(cn)

    out = _dot(rec_out.astype(BF16), w_out_ref[0:d_rec, :]) + _dot(c2.astype(BF16), w_out_ref[d_rec:d_rec + d_conv, :])
    y_tile = _layer_norm(alpha * x + out, lng_ref[...], lnb_ref[...])
    y_ref[...] = pltpu.einshape("tbd->btd", y_tile.reshape(tt, bsz, d_model))

    new_rc = xr_buf[rows:rows + rc_rows, :]
    new_cf = g_buf[rows:rows + cf_rows, :]
    xr_buf[0:rc_rows, :] = new_rc
    g_buf[0:cf_rows, :] = new_cf
    rc_out_ref[...] = new_rc.reshape(REC_CONV - 1, bsz, d_rec)
    cf_out_ref[...] = new_cf.reshape(CONV_WIDTH - 1, bsz, d_conv)
    h_out_ref[...] = h


def _const_spec(arr):
    zeros = (0,) * arr.ndim
    return pl.BlockSpec(arr.shape, lambda b, t: zeros)


def _mixer_ab_call(x, rc0, h0, cf0, prm, ln_g, ln_b, alpha, tt):
    bsz, seq, d_model = x.shape
    (w_in, rcw, rcb, wgate, gab, gxb, lam, cfw, cfb, cfg, cfbeta, w_out) = prm
    d_rec, d_conv = rcw.shape[1], cfw.shape[1]
    assert bsz == SUBLANES and seq % tt == 0 and (bsz * tt) % CONV_ROWS == 0
    consts = (w_in, rcw, rcb, wgate, gab, gxb, lam, cfw, cfb, cfg, cfbeta, w_out, ln_g, ln_b)
    states = (jnp.swapaxes(rc0, 0, 1), h0, jnp.swapaxes(cf0, 0, 1))

    def whole(arr):
        zeros = (0,) * arr.ndim
        return pl.BlockSpec(arr.shape, lambda s: zeros)

    state_shapes = (
        jax.ShapeDtypeStruct((REC_CONV - 1, bsz, d_rec), F32),
        jax.ShapeDtypeStruct((bsz, d_rec), F32),
        jax.ShapeDtypeStruct((CONV_WIDTH - 1, bsz, d_conv), F32),
    )
    rows = bsz * tt
    scratch = [
        pltpu.VMEM(((REC_CONV - 1) * bsz + rows, d_rec), F32),
        pltpu.VMEM(((CONV_WIDTH - 1) * bsz + rows, d_conv), F32),
        pltpu.VMEM((rows, d_rec), F32),
        pltpu.VMEM((rows, d_rec), F32),
        pltpu.VMEM((rows, d_rec), F32),
        pltpu.VMEM((bsz, d_rec), F32),
        pltpu.VMEM((rows, d_conv), F32),
        pltpu.VMEM((CONV_WIDTH, SUBLANES, d_conv), F32),
    ]
    y, rc, h, cf = pl.pallas_call(
        functools.partial(_mixer_ab_body, alpha),
        grid=(seq // tt,),
        in_specs=[pl.BlockSpec((bsz, tt, d_model), lambda s: (0, s, 0))]
        + [whole(a) for a in states] + [whole(c) for c in consts],
        out_specs=(pl.BlockSpec((bsz, tt, d_model), lambda s: (0, s, 0)),) + tuple(whole(a) for a in state_shapes),
        out_shape=(jax.ShapeDtypeStruct((bsz, seq, d_model), F32),) + state_shapes,
        scratch_shapes=scratch,
        compiler_params=pltpu.CompilerParams(
            dimension_semantics=("arbitrary",), vmem_limit_bytes=VMEM_LIMIT_BYTES),
        name="mixer_ab",
    )(x, *states, *consts)
    return y, jnp.swapaxes(rc, 0, 1), h, jnp.swapaxes(cf, 0, 1)


def _mixer_c_body(alpha, sink_ref, x_ref, k0_ref, v0_ref, wqkv_ref, w_out_ref, lng_ref, lnb_ref,
                  y_ref, k_out_ref, v_out_ref,
                  q_buf, k_buf, v_buf, o_buf, bias_buf, sink_buf):
    tt = x_ref.shape[0]
    d_q = N_HEADS * HEAD_DIM
    d_kv = N_KV * HEAD_DIM
    win_keys = WINDOW + CHUNK
    stack = GROUP * CHUNK
    t = pl.program_id(1)

    @pl.when(t == 0)
    def _():
        k_buf[0:WINDOW, :] = k0_ref[...].astype(BF16)
        v_buf[0:WINDOW, :] = v0_ref[...].astype(BF16)
        row = lax.broadcasted_iota(jnp.int32, (stack, win_keys), 0)
        ko = lax.broadcasted_iota(jnp.int32, (stack, win_keys), 1)
        absdist = jnp.abs(WINDOW + (row % CHUNK) - ko).astype(F32)
        row1 = lax.broadcasted_iota(jnp.int32, (stack, 1), 0)
        for kh in range(N_KV):
            slope = jnp.zeros((stack, win_keys), F32)
            sink = jnp.zeros((stack, 1), F32)
            for g in range(GROUP):
                h = kh * GROUP + g
                slope = jnp.where(row // CHUNK == g, 2.0 ** (-8.0 * (h + 1) / N_HEADS), slope)
                sink = jnp.where(row1 // CHUNK == g, sink_ref[h], sink)
            bias_buf[kh] = -slope * absdist
            sink_buf[kh] = sink

    x = x_ref[...]
    xb = x.astype(BF16)
    q_buf[...] = (_dot(xb, wqkv_ref[:, 0:d_q]) * (HEAD_DIM ** -0.5)).astype(BF16)
    k = _dot(xb, wqkv_ref[:, d_q:d_q + d_kv])
    v = _dot(xb, wqkv_ref[:, d_q + d_kv:d_q + 2 * d_kv])
    k_buf[WINDOW:WINDOW + tt, :] = k.astype(BF16)
    v_buf[WINDOW:WINDOW + tt, :] = v.astype(BF16)
    if tt >= WINDOW:
        k_out_ref[...] = k[tt - WINDOW:tt, :]
        v_out_ref[...] = v[tt - WINDOW:tt, :]
    else:
        k_out_ref[0:WINDOW - tt, :] = k0_ref[tt:WINDOW, :]
        v_out_ref[0:WINDOW - tt, :] = v0_ref[tt:WINDOW, :]
        k_out_ref[WINDOW - tt:WINDOW, :] = k
        v_out_ref[WINDOW - tt:WINDOW, :] = v

    def chunk_body(c, carry):
        r0 = pl.multiple_of(c * CHUNK, CHUNK)
        scores = []
        for kh in range(N_KV):
            q4 = q_buf[pl.ds(r0, CHUNK), kh * stack:(kh + 1) * stack]
            qs = jnp.concatenate([q4[:, g * HEAD_DIM:(g + 1) * HEAD_DIM] for g in range(GROUP)], axis=0)
            kwh = k_buf[pl.ds(r0, win_keys), kh * HEAD_DIM:(kh + 1) * HEAD_DIM]
            s = lax.dot_general(qs, kwh, (((1,), (1,)), ((), ())), preferred_element_type=F32)
            scores.append(s + bias_buf[kh])
        probs, denoms = [], []
        for kh in range(N_KV):
            s = scores[kh]
            sink = sink_buf[kh]
            m = jnp.maximum(jnp.max(s, axis=-1, keepdims=True), sink)
            p = jnp.exp(s - m)
            denoms.append(jnp.sum(p, axis=-1, keepdims=True) + jnp.exp(sink - m))
            probs.append(p.astype(BF16))
        for kh in range(N_KV):
            vwh = v_buf[pl.ds(r0, win_keys), kh * HEAD_DIM:(kh + 1) * HEAD_DIM]
            o = (_dot(probs[kh], vwh) / denoms[kh]).astype(BF16)
            for g in range(GROUP):
                h = kh * GROUP + g
                o_buf[pl.ds(r0, CHUNK), h * HEAD_DIM:(h + 1) * HEAD_DIM] = o[g * CHUNK:(g + 1) * CHUNK, :]
        return carry

    lax.fori_loop(0, tt // CHUNK, chunk_body, 0)

    out = _dot(o_buf[...], w_out_ref[...])
    y_ref[...] = _layer_norm(alpha * x + out, lng_ref[...], lnb_ref[...])
    k_buf[0:WINDOW, :] = k_buf[tt:tt + WINDOW, :]
    v_buf[0:WINDOW, :] = v_buf[tt:tt + WINDOW, :]


def _mixer_c_call(x, k0, v0, wqkv, sinks, w_out, ln_g, ln_b, alpha, tt):
    bsz, seq, d_model = x.shape
    d_q = N_HEADS * HEAD_DIM
    d_kv = N_KV * HEAD_DIM
    assert seq % tt == 0 and tt % CHUNK == 0
    assert tt >= WINDOW or seq == tt
    grid = (bsz, seq // tt)
    consts = (wqkv, w_out, ln_g, ln_b)
    in_specs = [
        pl.BlockSpec(memory_space=pltpu.SMEM),
        pl.BlockSpec((None, tt, d_model), lambda b, t: (b, t, 0)),
        pl.BlockSpec((None, WINDOW, d_kv), lambda b, t: (b, 0, 0)),
        pl.BlockSpec((None, WINDOW, d_kv), lambda b, t: (b, 0, 0)),
    ] + [_const_spec(c) for c in consts]
    out_shape = (
        jax.ShapeDtypeStruct((bsz, seq, d_model), F32),
        jax.ShapeDtypeStruct((bsz, WINDOW, d_kv), F32),
        jax.ShapeDtypeStruct((bsz, WINDOW, d_kv), F32),
    )
    out_specs = (
        pl.BlockSpec((None, tt, d_model), lambda b, t: (b, t, 0)),
        pl.BlockSpec((None, WINDOW, d_kv), lambda b, t: (b, 0, 0)),
        pl.BlockSpec((None, WINDOW, d_kv), lambda b, t: (b, 0, 0)),
    )
    scratch = [
        pltpu.VMEM((tt, d_q), BF16),
        pltpu.VMEM((tt + WINDOW, d_kv), BF16),
        pltpu.VMEM((tt + WINDOW, d_kv), BF16),
        pltpu.VMEM((tt, d_q), BF16),
        pltpu.VMEM((N_KV, GROUP * CHUNK, WINDOW + CHUNK), F32),
        pltpu.VMEM((N_KV, GROUP * CHUNK, 1), F32),
    ]
    return pl.pallas_call(
        functools.partial(_mixer_c_body, alpha),
        grid=grid, in_specs=in_specs, out_specs=out_specs, out_shape=out_shape,
        scratch_shapes=scratch,
        compiler_params=pltpu.CompilerParams(
            dimension_semantics=("arbitrary", "arbitrary"), vmem_limit_bytes=VMEM_LIMIT_BYTES),
        name="mixer_c",
    )(sinks, x, k0, v0, *consts)


QBLK = 2 * CHUNK
KBLK = WINDOW + QBLK
ONES_ROWS = 16
ATTN_SKEW = 4


def _mixer_c_prompt_body(alpha, sink_ref, x_ref, wqT_ref, wk_ref, wv_ref, wvT_ref, w_out_ref, lng_ref, lnb_ref,
                         y_ref, k_out_ref, v_out_ref,
                         qT_buf, k_buf, vT_buf, oT_buf, bias_buf, sink_buf):
    tt = x_ref.shape[0]
    t = pl.program_id(1)
    n_pairs = N_HEADS // 2
    lanes = 2 * QBLK

    @pl.when(t == 0)
    def _():
        vT_buf[...] = jnp.ones(vT_buf.shape, BF16)
        k_buf[:, 0:WINDOW, :] = jnp.zeros((N_KV, WINDOW, HEAD_DIM), BF16)
        kr = lax.broadcasted_iota(jnp.int32, (KBLK, lanes), 0)
        ln = lax.broadcasted_iota(jnp.int32, (KBLK, lanes), 1)
        ql = ln % QBLK
        absdist = jnp.abs(ql + WINDOW - kr).astype(F32)
        kc, qc = kr // CHUNK, ql // CHUNK
        in_window = (kc >= qc) & (kc <= qc + WINDOW // CHUNK)
        ln1 = lax.broadcasted_iota(jnp.int32, (SUBLANES, lanes), 1)
        for pr in range(n_pairs):
            s0 = 2.0 ** (-8.0 * (2 * pr + 1) / N_HEADS)
            s1 = 2.0 ** (-8.0 * (2 * pr + 2) / N_HEADS)
            slope = jnp.where(ln < QBLK, s0, s1)
            bias_buf[pr] = jnp.where(in_window, -slope * absdist, NEG_INF)
            sink_buf[pr] = jnp.where(ln1 < QBLK, sink_ref[2 * pr], sink_ref[2 * pr + 1])

    x = x_ref[...]
    xb = x.astype(BF16)
    nt_dims = (((1,), (1,)), ((), ()))
    qT = lax.dot_general(wqT_ref[...], xb, nt_dims, preferred_element_type=F32)
    qT_buf[...] = (qT * (HEAD_DIM ** -0.5)).astype(BF16)
    k = _dot(xb, wk_ref[...])
    kb = k.astype(BF16)
    vT = lax.dot_general(wvT_ref[...], xb, nt_dims, preferred_element_type=F32)
    for kh in range(N_KV):
        k_buf[kh, WINDOW:WINDOW + tt, :] = kb[:, kh * HEAD_DIM:(kh + 1) * HEAD_DIM]
        vT_buf[kh, 0:HEAD_DIM, WINDOW:WINDOW + tt] = vT[kh * HEAD_DIM:(kh + 1) * HEAD_DIM, :].astype(BF16)
    k_out_ref[...] = k[tt - WINDOW:tt, :]

    @pl.when(t == pl.num_programs(1) - 1)
    def _():
        v_out_ref[...] = _dot(xb[tt - WINDOW:tt, :], wv_ref[...])

    def scores(qb, pr, key_lo):
        kh = pr // (GROUP // 2)
        q_lanes = slice(qb * QBLK, (qb + 1) * QBLK)
        h_rows = [slice((2 * pr + i) * HEAD_DIM, (2 * pr + i + 1) * HEAD_DIM) for i in range(2)]
        q2 = jnp.concatenate([qT_buf[h_rows[0], q_lanes], qT_buf[h_rows[1], q_lanes]], axis=1)
        keys = slice(qb * QBLK + key_lo, qb * QBLK + KBLK)
        return _dot(k_buf[kh, keys, :], q2) + bias_buf[pr, key_lo:KBLK, :]

    def finish(qb, pr, key_lo, sT):
        kh = pr // (GROUP // 2)
        q_lanes = slice(qb * QBLK, (qb + 1) * QBLK)
        h_rows = [slice((2 * pr + i) * HEAD_DIM, (2 * pr + i + 1) * HEAD_DIM) for i in range(2)]
        keys = slice(qb * QBLK + key_lo, qb * QBLK + KBLK)
        sink = sink_buf[pr][0:1, :]
        m = jnp.maximum(jnp.max(sT, axis=0, keepdims=True), sink)
        p = jnp.exp(sT - m)
        pv = _dot(vT_buf[kh, :, keys], p.astype(BF16))
        den = pv[HEAD_DIM:HEAD_DIM + 1, :] + jnp.exp(sink - m)
        oT = (pv[0:HEAD_DIM, :] * (1.0 / den)).astype(BF16)
        oT_buf[h_rows[0], q_lanes] = oT[:, 0:QBLK]
        oT_buf[h_rows[1], q_lanes] = oT[:, QBLK:lanes]

    def attend_all(first_key_lo):
        items = [(qb, pr, first_key_lo if qb == 0 else 0) for qb in range(tt // QBLK) for pr in range(n_pairs)]
        pending = []
        for item in items:
            pending.append((item, scores(*item)))
            if len(pending) > ATTN_SKEW:
                done, sT = pending.pop(0)
                finish(*done, sT)
        for done, sT in pending:
            finish(*done, sT)

    @pl.when(t == 0)
    def _():
        attend_all(WINDOW)

    @pl.when(t != 0)
    def _():
        attend_all(0)

    out = lax.dot_general(oT_buf[...], w_out_ref[...], (((0,), (0,)), ((), ())), preferred_element_type=F32)
    y_ref[...] = _layer_norm(alpha * x + out, lng_ref[...], lnb_ref[...])
    k_buf[:, 0:WINDOW, :] = k_buf[:, tt:tt + WINDOW, :]
    vT_buf[:, 0:HEAD_DIM, 0:WINDOW] = vT_buf[:, 0:HEAD_DIM, tt:tt + WINDOW]


def _mixer_c_prompt_call(x, wqT, wk, wv, wvT, sinks, w_out, ln_g, ln_b, alpha, tt):
    bsz, seq, d_model = x.shape
    d_q = N_HEADS * HEAD_DIM
    d_kv = N_KV * HEAD_DIM
    assert seq % tt == 0 and tt % QBLK == 0 and tt >= WINDOW
    consts = (wqT, wk, wv, wvT, w_out, ln_g, ln_b)
    in_specs = [
        pl.BlockSpec(memory_space=pltpu.SMEM),
        pl.BlockSpec((None, tt, d_model), lambda b, t: (b, t, 0)),
    ] + [_const_spec(c) for c in consts]
    out_shape = (
        jax.ShapeDtypeStruct((bsz, seq, d_model), F32),
        jax.ShapeDtypeStruct((bsz, WINDOW, d_kv), F32),
        jax.ShapeDtypeStruct((bsz, WINDOW, d_kv), F32),
    )
    out_specs = (
        pl.BlockSpec((None, tt, d_model), lambda b, t: (b, t, 0)),
        pl.BlockSpec((None, WINDOW, d_kv), lambda b, t: (b, 0, 0)),
        pl.BlockSpec((None, WINDOW, d_kv), lambda b, t: (b, 0, 0)),
    )
    scratch = [
        pltpu.VMEM((d_q, tt), BF16),
        pltpu.VMEM((N_KV, tt + WINDOW, HEAD_DIM), BF16),
        pltpu.VMEM((N_KV, HEAD_DIM + ONES_ROWS, tt + WINDOW), BF16),
        pltpu.VMEM((d_q, tt), BF16),
        pltpu.VMEM((N_HEADS // 2, KBLK, 2 * QBLK), F32),
        pltpu.VMEM((N_HEADS // 2, SUBLANES, 2 * QBLK), F32),
    ]
    return pl.pallas_call(
        functools.partial(_mixer_c_prompt_body, alpha),
        grid=(bsz, seq // tt), in_specs=in_specs, out_specs=out_specs, out_shape=out_shape,
        scratch_shapes=scratch,
        compiler_params=pltpu.CompilerParams(
            dimension_semantics=("arbitrary", "arbitrary"), vmem_limit_bytes=VMEM_LIMIT_BYTES),
        name="mixer_c_prompt",
    )(sinks, x, *consts)


FFN_UNROLL = 5


def _ffn_body(alpha, x_ref, wg_ref, wu_ref, wd_ref, lng_ref, lnb_ref, y_ref, xb_buf, acc_buf):
    n_chunks = wg_ref.shape[0]
    xb_buf[...] = x_ref[...].astype(BF16)

    def down(c):
        xb = xb_buf[...]
        g = _dot(xb, wg_ref[c])
        u = _dot(xb, wu_ref[c])
        hid = (g * _sigmoid(g) * u).astype(BF16)
        return _dot(hid, wd_ref[c])

    acc_buf[...] = alpha * x_ref[...] + down(0)

    def chunk_body(c, carry):
        acc_buf[...] += down(c)
        return carry

    lax.fori_loop(1, n_chunks, chunk_body, 0, unroll=FFN_UNROLL)
    y_ref[...] = _layer_norm(acc_buf[...], lng_ref[...], lnb_ref[...])


def _ffn_call(x2d, layer, wg, wu, wd, ln_g, ln_b, alpha, tm):
    rows, d_model = x2d.shape
    assert rows % tm == 0
    consts = (wg, wu, wd, ln_g, ln_b)

    def const_spec(arr):
        zeros = (0,) * (arr.ndim - 1)
        return pl.BlockSpec((None,) + arr.shape[1:], lambda i: (layer,) + zeros, pipeline_mode=pl.Buffered(1))

    return pl.pallas_call(
        functools.partial(_ffn_body, alpha),
        grid=(rows // tm,),
        in_specs=[pl.BlockSpec((tm, d_model), lambda i: (i, 0))] + [const_spec(c) for c in consts],
        out_specs=pl.BlockSpec((tm, d_model), lambda i: (i, 0)),
        out_shape=jax.ShapeDtypeStruct((rows, d_model), F32),
        scratch_shapes=[pltpu.VMEM((tm, d_model), BF16), pltpu.VMEM((tm, d_model), F32)],
        compiler_params=pltpu.CompilerParams(
            dimension_semantics=("arbitrary",), vmem_limit_bytes=VMEM_LIMIT_BYTES),
        name="ffn",
    )(x2d, *consts)


def _ffn(x, layer, ffn_prm, alpha, tm):
    d_model = x.shape[-1]
    return _ffn_call(x.reshape(-1, d_model), layer, *ffn_prm, alpha, tm).reshape(x.shape)


def _row(v):
    return v.reshape(1, -1).astype(F32)


def _gate_tiles(ga_w, gx_w):
    heads, blk, _ = ga_w.shape
    per_tile = MXU_DIM // blk
    tiles = []
    for j in range(heads // per_tile):
        def diag(w):
            return jax.scipy.linalg.block_diag(*[w[j * per_tile + i] for i in range(per_tile)])
        tiles.append(jnp.concatenate([diag(ga_w), diag(gx_w)], axis=1))
    return jnp.stack(tiles).astype(BF16)


def _ffn_chunks(wg, wu, wd):
    depth, d_model, d_ff = wg.shape
    n = d_ff // MXU_DIM
    wg_c = wg.astype(BF16).reshape(depth, d_model, n, MXU_DIM).transpose(0, 2, 1, 3)
    wu_c = wu.astype(BF16).reshape(depth, d_model, n, MXU_DIM).transpose(0, 2, 1, 3)
    wd_c = wd.astype(BF16).reshape(depth, n, MXU_DIM, d_model)
    return wg_c, wu_c, wd_c


def kernel(x_prompt, x_sample, state_rec_h, state_rec_conv, state_cf_conv, cache_k, cache_v, w_in_ab, rec_conv_w, rec_conv_b, rec_gate_a_w, rec_gate_a_b, rec_gate_x_w, rec_gate_x_b, rec_lambda, cf_conv_w, cf_conv_b, cf_norm_g, cf_norm_b, w_out_ab, w_qkv, attn_sinks, w_out_c, ln_mix_g, ln_mix_b, w_ff_gate, w_ff_up, w_ff_down, ln_ff_g, ln_ff_b):
    depth = ln_mix_g.shape[0]
    alpha = (2 * depth) ** 0.25
    bp, seq_p, d_model = x_prompt.shape
    bs, seq_s, _ = x_sample.shape
    d_rec = rec_conv_w.shape[-1]
    d_conv = cf_conv_w.shape[-1]
    d_kv = N_KV * HEAD_DIM
    tt_ab = min(seq_p, 64)
    tt_c = min(seq_p, 512)
    tt_s = seq_s
    tm_p = min(bp * seq_p, 1024)
    tm_s = bs * seq_s

    ffn_prm = _ffn_chunks(w_ff_gate, w_ff_up, w_ff_down) + (
        ln_ff_g.reshape(depth, 1, d_model).astype(F32), ln_ff_b.reshape(depth, 1, d_model).astype(F32))

    yp, ys = x_prompt, x_sample
    p_h, s_h, p_rc, s_rc, p_cf, s_cf, p_k, s_k, p_v, s_v = ([] for _ in range(10))
    for layer in range(depth):
        j = layer // 2
        ln_g, ln_b = _row(ln_mix_g[layer]), _row(ln_mix_b[layer])
        if layer % 2 == 0:
            prm =(w_in_ab[j].astype(BF16), rec_conv_w[j], _row(rec_conv_b[j]),
                   _gate_tiles(rec_gate_a_w[j], rec_gate_x_w[j]), _row(rec_gate_a_b[j]), _row(rec_gate_x_b[j]),
                   _row(rec_lambda[j]), cf_conv_w[j], _row(cf_conv_b[j]), _row(cf_norm_g[j]), _row(cf_norm_b[j]),
                   w_out_ab[j].astype(BF16))
            yp, rc, hl, cf = _mixer_ab_call(
                yp, jnp.zeros((bp, REC_CONV - 1, d_rec), F32), jnp.zeros((bp, d_rec), F32),
                jnp.zeros((bp, CONV_WIDTH - 1, d_conv), F32), prm, ln_g, ln_b, alpha, tt_ab)
            p_rc.append(rc), p_h.append(hl), p_cf.append(cf)
            ys, rc, hl, cf = _mixer_ab_call(
                ys, state_rec_conv[j], state_rec_h[j], state_cf_conv[j], prm, ln_g, ln_b, alpha, tt_s)
            s_rc.append(rc), s_h.append(hl), s_cf.append(cf)
        else:
            wqkv, w_out = w_qkv[j].astype(BF16), w_out_c[j].astype(BF16)
            sinks = attn_sinks[j].astype(F32)
            d_q = N_HEADS * HEAD_DIM
            wq_t = wqkv[:, 0:d_q].T
            wk, wv = wqkv[:, d_q:d_q + d_kv], wqkv[:, d_q + d_kv:d_q + 2 * d_kv]
            yp, kk, vv = _mixer_c_prompt_call(yp, wq_t, wk, wv, wv.T, sinks, w_out, ln_g, ln_b, alpha, tt_c)
            p_k.append(kk.reshape(bp, WINDOW, N_KV, HEAD_DIM)), p_v.append(vv.reshape(bp, WINDOW, N_KV, HEAD_DIM))
            ck = cache_k[j].reshape(bs, WINDOW, d_kv)
            cv = cache_v[j].reshape(bs, WINDOW, d_kv)
            ys, kk, vv = _mixer_c_call(ys, ck, cv, wqkv, sinks, w_out, ln_g, ln_b, alpha, tt_s)
            s_k.append(kk.reshape(bs, WINDOW, N_KV, HEAD_DIM)), s_v.append(vv.reshape(bs, WINDOW, N_KV, HEAD_DIM))
        yp = _ffn(yp, layer, ffn_prm, alpha, tm_p)
        ys = _ffn(ys, layer, ffn_prm, alpha, tm_s)
    return (yp, ys, jnp.stack(p_h), jnp.stack(s_h), jnp.stack(p_rc), jnp.stack(s_rc),
            jnp.stack(p_cf), jnp.stack(s_cf), jnp.stack(p_k), jnp.stack(s_k), jnp.stack(p_v), jnp.stack(s_v))
```

```python
import functools

import jax
import jax.numpy as jnp
from jax import lax
from jax.experimental import pallas as pl
from jax.experimental.pallas import tpu as pltpu

CHUNK = 64
WINDOW = 128
HEAD_DIM = 64
N_HEADS = 16
N_KV = 4
GROUP = N_HEADS // N_KV
REC_CONV = 4
CONV_WIDTH = 31
LRU_C = 8.0
LN_EPS = 1e-5
NEG_INF = -1e30

SUBLANES = 8
MXU_DIM = 256
CONV_ROWS = 16
VMEM_LIMIT_BYTES = 56 * 1024 * 1024

BF16 = jnp.bfloat16
F32 = jnp.float32


def _dot(a, b):
    return jnp.dot(a, b, preferred_element_type=F32)


def _layer_norm(z, g, b):
    mu = jnp.mean(z, axis=-1, keepdims=True)
    zc = z - mu
    var = jnp.mean(zc * zc, axis=-1, keepdims=True)
    return zc * lax.rsqrt(var + LN_EPS) * g + b


def _sigmoid(x):
    return jax.nn.sigmoid(x)


def _mixer_ab_body(alpha, x_ref, rc0_ref, h0_ref, cf0_ref, w_in_ref, rcw_ref, rcb_ref, wgate_ref,
                   gab_ref, gxb_ref, lam_ref, cfw_ref, cfb_ref, cfg_ref, cfbeta_ref, w_out_ref,
                   lng_ref, lnb_ref,
                   y_ref, rc_out_ref, h_out_ref, cf_out_ref,
                   xr_buf, g_buf, a_buf, b_buf, h_buf, hcar, c_buf, wb_buf):
    bsz, tt, d_model = x_ref.shape
    rows = bsz * tt
    d_rec = rcw_ref.shape[1]
    d_conv = cfw_ref.shape[1]
    rc_rows = (REC_CONV - 1) * bsz
    cf_rows = (CONV_WIDTH - 1) * bsz
    step = pl.program_id(0)

    @pl.when(step == 0)
    def _():
        xr_buf[0:rc_rows, :] = rc0_ref[...].reshape(rc_rows, d_rec)
        g_buf[0:cf_rows, :] = cf0_ref[...].reshape(cf_rows, d_conv)
        hcar[...] = h0_ref[...]
        for k in range(CONV_WIDTH):
            wb_buf[k] = jnp.broadcast_to(cfw_ref[k:k + 1, :], (SUBLANES, d_conv))

    x = pltpu.einshape("btd->tbd", x_ref[...]).reshape(rows, d_model)
    xb = x.astype(BF16)

    xr = _dot(xb, w_in_ref[:, 0:d_rec])
    yr = _dot(xb, w_in_ref[:, d_rec:2 * d_rec])
    cv = _dot(xb, w_in_ref[:, 2 * d_rec:2 * d_rec + d_conv])
    cg = _dot(xb, w_in_ref[:, 2 * d_rec + d_conv:2 * d_rec + 2 * d_conv])

    xr_buf[rc_rows:rc_rows + rows, :] = xr
    xc = rcb_ref[...] + rcw_ref[REC_CONV - 1:REC_CONV, :] * xr
    for k in range(REC_CONV - 1):
        xc = xc + rcw_ref[k:k + 1, :] * xr_buf[k * bsz:k * bsz + rows, :]
    xcb = xc.astype(BF16)
    r_parts, i_parts = [], []
    for j in range(d_rec // MXU_DIM):
        lo, hi = j * MXU_DIM, (j + 1) * MXU_DIM
        gj = _dot(xcb[:, lo:hi], wgate_ref[j])
        r_parts.append(_sigmoid(gj[:, :MXU_DIM] + gab_ref[:, lo:hi]))
        i_parts.append(_sigmoid(gj[:, MXU_DIM:] + gxb_ref[:, lo:hi]))
    r = jnp.concatenate(r_parts, axis=1)
    i = jnp.concatenate(i_parts, axis=1)
    nlam = -lam_ref[...]
    softplus = jnp.maximum(nlam, 0.0) + jnp.log1p(jnp.exp(-jnp.abs(nlam)))
    log_a = (-LRU_C * softplus) * r
    a = jnp.exp(log_a)
    y = 1.0 - a * a
    a_buf[...] = a
    b_buf[...] = jnp.where(y > 0.0, y * lax.rsqrt(y), 0.0) * (i * xc)

    h = hcar[...]
    for t in range(tt):
        grp = slice(t * bsz, (t + 1) * bsz)
        h = a_buf[grp, :] * h + b_buf[grp, :]
        h_buf[grp, :] = h
    hcar[...] = h

    rec_out = h_buf[...] * jax.nn.gelu(yr)

    g_buf[cf_rows:cf_rows + rows, :] = cv * _sigmoid(cg)
    blk_groups = CONV_ROWS // SUBLANES

    def conv_block(rb, carry):
        row0 = pl.multiple_of(rb * CONV_ROWS, CONV_ROWS)
        acc = jnp.broadcast_to(cfb_ref[...], (CONV_ROWS, d_conv)).reshape(blk_groups, SUBLANES, d_conv)
        for k in range(CONV_WIDTH):
            src = g_buf[pl.ds(row0 + k * bsz, CONV_ROWS), :]
            acc = acc + wb_buf[k] * src.reshape(blk_groups, SUBLANES, d_conv)
        c_buf[pl.ds(row0, CONV_ROWS), :] = acc.reshape(CONV_ROWS, d_conv)
        return carry

    lax.fori_loop(0, rows // CONV_ROWS, conv_block, 0, unroll=True)
    cn = _layer_norm(c_buf[...], cfg_ref[...], cfbeta_ref[...])
    c2 = cn * _sigmoid(cn)

    out = _dot(rec_out.astype(BF16), w_out_ref[0:d_rec, :]) + _dot(c2.astype(BF16), w_out_ref[d_rec:d_rec + d_conv, :])
    y_tile = _layer_norm(alpha * x + out, lng_ref[...], lnb_ref[...])
    y_ref[...] = pltpu.einshape("tbd->btd", y_tile.reshape(tt, bsz, d_model))

    new_rc = xr_buf[rows:rows + rc_rows, :]
    new_cf = g_buf[rows:rows + cf_rows, :]
    xr_buf[0:rc_rows, :] = new_rc
    g_buf[0:cf_rows, :] = new_cf
    rc_out_ref[...] = new_rc.reshape(REC_CONV - 1, bsz, d_rec)
    cf_out_ref[...] = new_cf.reshape(CONV_WIDTH - 1, bsz, d_conv)
    h_out_ref[...] = h


def _const_spec(arr):
    zeros = (0,) * arr.ndim
    return pl.BlockSpec(arr.shape, lambda b, t: zeros)


def _mixer_ab_call(x, rc0, h0, cf0, prm, ln_g, ln_b, alpha, tt):
    bsz, seq, d_model = x.shape
    (w_in, rcw, rcb, wgate, gab, gxb, lam, cfw, cfb, cfg, cfbeta, w_out) = prm
    d_rec, d_conv = rcw.shape[1], cfw.shape[1]
    assert bsz == SUBLANES and seq % tt == 0 and (bsz * tt) % CONV_ROWS == 0
    consts = (w_in, rcw, rcb, wgate, gab, gxb, lam, cfw, cfb, cfg, cfbeta, w_out, ln_g, ln_b)
    states = (jnp.swapaxes(rc0, 0, 1), h0, jnp.swapaxes(cf0, 0, 1))

    def whole(arr):
        zeros = (0,) * arr.ndim
        return pl.BlockSpec(arr.shape, lambda s: zeros)

    state_shapes = (
        jax.ShapeDtypeStruct((REC_CONV - 1, bsz, d_rec), F32),
        jax.ShapeDtypeStruct((bsz, d_rec), F32),
        jax.ShapeDtypeStruct((CONV_WIDTH - 1, bsz, d_conv), F32),
    )
    rows = bsz * tt
    scratch = [
        pltpu.VMEM(((REC_CONV - 1) * bsz + rows, d_rec), F32),
        pltpu.VMEM(((CONV_WIDTH - 1) * bsz + rows, d_conv), F32),
        pltpu.VMEM((rows, d_rec), F32),
        pltpu.VMEM((rows, d_rec), F32),
        pltpu.VMEM((rows, d_rec), F32),
        pltpu.VMEM((bsz, d_rec), F32),
        pltpu.VMEM((rows, d_conv), F32),
        pltpu.VMEM((CONV_WIDTH, SUBLANES, d_conv), F32),
    ]
    y, rc, h, cf = pl.pallas_call(
        functools.partial(_mixer_ab_body, alpha),
        grid=(seq // tt,),
        in_specs=[pl.BlockSpec((bsz, tt, d_model), lambda s: (0, s, 0))]
        + [whole(a) for a in states] + [whole(c) for c in consts],
        out_specs=(pl.BlockSpec((bsz, tt, d_model), lambda s: (0, s, 0)),) + tuple(whole(a) for a in state_shapes),
        out_shape=(jax.ShapeDtypeStruct((bsz, seq, d_model), F32),) + state_shapes,
        scratch_shapes=scratch,
        compiler_params=pltpu.CompilerParams(
            dimension_semantics=("arbitrary",), vmem_limit_bytes=VMEM_LIMIT_BYTES),
        name="mixer_ab",
    )(x, *states, *consts)
    return y, jnp.swapaxes(rc, 0, 1), h, jnp.swapaxes(cf, 0, 1)


def _mixer_c_body(alpha, sink_ref, x_ref, k0_ref, v0_ref, wqkv_ref, w_out_ref, lng_ref, lnb_ref,
                  y_ref, k_out_ref, v_out_ref,
                  q_buf, k_buf, v_buf, o_buf, bias_buf, sink_buf):
    bsz, tt, d_model = x_ref.shape
    rows = bsz * tt
    d_q = N_HEADS * HEAD_DIM
    d_kv = N_KV * HEAD_DIM
    win_keys = WINDOW + CHUNK
    stack = GROUP * CHUNK

    row = lax.broadcasted_iota(jnp.int32, (stack, win_keys), 0)
    ko = lax.broadcasted_iota(jnp.int32, (stack, win_keys), 1)
    absdist = jnp.abs(WINDOW + (row % CHUNK) - ko).astype(F32)
    row1 = lax.broadcasted_iota(jnp.int32, (stack, 1), 0)
    for kh in range(N_KV):
        slope = jnp.zeros((stack, win_keys), F32)
        sink = jnp.zeros((stack, 1), F32)
        for g in range(GROUP):
            h = kh * GROUP + g
            slope = jnp.where(row // CHUNK == g, 2.0 ** (-8.0 * (h + 1) / N_HEADS), slope)
            sink = jnp.where(row1 // CHUNK == g, sink_ref[h], sink)
        bias_buf[kh] = -slope * absdist
        sink_buf[kh] = sink

    x = x_ref[...].reshape(rows, d_model)
    xb = x.astype(BF16)
    q_buf[...] = (_dot(xb, wqkv_ref[:, 0:d_q]) * (HEAD_DIM ** -0.5)).astype(BF16)
    k = _dot(xb, wqkv_ref[:, d_q:d_q + d_kv]).reshape(bsz, tt, d_kv)
    v = _dot(xb, wqkv_ref[:, d_q + d_kv:d_q + 2 * d_kv]).reshape(bsz, tt, d_kv)
    k_buf[:, 0:WINDOW, :] = k0_ref[...].astype(BF16)
    v_buf[:, 0:WINDOW, :] = v0_ref[...].astype(BF16)
    k_buf[:, WINDOW:win_keys, :] = k.astype(BF16)
    v_buf[:, WINDOW:win_keys, :] = v.astype(BF16)
    k_out_ref[:, 0:WINDOW - tt, :] = k0_ref[:, tt:WINDOW, :]
    v_out_ref[:, 0:WINDOW - tt, :] = v0_ref[:, tt:WINDOW, :]
    k_out_ref[:, WINDOW - tt:WINDOW, :] = k
    v_out_ref[:, WINDOW - tt:WINDOW, :] = v

    def seq_body(b, carry):
        r0 = pl.multiple_of(b * CHUNK, CHUNK)
        scores = []
        for kh in range(N_KV):
            q4 = q_buf[pl.ds(r0, CHUNK), kh * stack:(kh + 1) * stack]
            qs = jnp.concatenate([q4[:, g * HEAD_DIM:(g + 1) * HEAD_DIM] for g in range(GROUP)], axis=0)
            kwh = k_buf[b, :, kh * HEAD_DIM:(kh + 1) * HEAD_DIM]
            s = lax.dot_general(qs, kwh, (((1,), (1,)), ((), ())), preferred_element_type=F32)
            scores.append(s + bias_buf[kh])
        probs, denoms = [], []
        for kh in range(N_KV):
            s = scores[kh]
            sink = sink_buf[kh]
            m = jnp.maximum(jnp.max(s, axis=-1, keepdims=True), sink)
            p = jnp.exp(s - m)
            denoms.append(jnp.sum(p, axis=-1, keepdims=True) + jnp.exp(sink - m))
            probs.append(p.astype(BF16))
        for kh in range(N_KV):
            vwh = v_buf[b, :, kh * HEAD_DIM:(kh + 1) * HEAD_DIM]
            o = (_dot(probs[kh], vwh) / denoms[kh]).astype(BF16)
            for g in range(GROUP):
                h = kh * GROUP + g
                o_buf[pl.ds(r0, CHUNK), h * HEAD_DIM:(h + 1) * HEAD_DIM] = o[g * CHUNK:(g + 1) * CHUNK, :]
        return carry

    lax.fori_loop(0, bsz, seq_body, 0)

    out = _dot(o_buf[...], w_out_ref[...])
    y_ref[...] = _layer_norm(alpha * x + out, lng_ref[...], lnb_ref[...]).reshape(bsz, tt, d_model)


def _mixer_c_call(x, k0, v0, wqkv, sinks, w_out, ln_g, ln_b, alpha):
    bsz, seq, d_model = x.shape
    d_q = N_HEADS * HEAD_DIM
    d_kv = N_KV * HEAD_DIM
    assert seq == CHUNK and k0.shape == (bsz, WINDOW, d_kv)
    operands = (x, k0, v0, wqkv, w_out, ln_g, ln_b)

    def whole(arr):
        zeros = (0,) * arr.ndim
        return pl.BlockSpec(arr.shape, lambda s: zeros)

    out_shape = (
        jax.ShapeDtypeStruct((bsz, seq, d_model), F32),
        jax.ShapeDtypeStruct((bsz, WINDOW, d_kv), F32),
        jax.ShapeDtypeStruct((bsz, WINDOW, d_kv), F32),
    )
    scratch = [
        pltpu.VMEM((bsz * seq, d_q), BF16),
        pltpu.VMEM((bsz, WINDOW + CHUNK, d_kv), BF16),
        pltpu.VMEM((bsz, WINDOW + CHUNK, d_kv), BF16),
        pltpu.VMEM((bsz * seq, d_q), BF16),
        pltpu.VMEM((N_KV, GROUP * CHUNK, WINDOW + CHUNK), F32),
        pltpu.VMEM((N_KV, GROUP * CHUNK, 1), F32),
    ]
    return pl.pallas_call(
        functools.partial(_mixer_c_body, alpha),
        grid=(1,),
        in_specs=[pl.BlockSpec(memory_space=pltpu.SMEM)] + [whole(a) for a in operands],
        out_specs=tuple(whole(a) for a in out_shape), out_shape=out_shape,
        scratch_shapes=scratch,
        compiler_params=pltpu.CompilerParams(
            dimension_semantics=("arbitrary",), vmem_limit_bytes=VMEM_LIMIT_BYTES),
        name="mixer_c",
    )(sinks, *operands)


QBLK = 2 * CHUNK
KBLK = WINDOW + QBLK
ONES_ROWS = 16
ATTN_SKEW = 4


def _mixer_c_prompt_body(alpha, sink_ref, x_ref, wqT_ref, wk_ref, wv_ref, wvT_ref, w_out_ref, lng_ref, lnb_ref,
                         y_ref, k_out_ref, v_out_ref,
                         qT_buf, k_buf, vT_buf, oT_buf, bias_buf, sink_buf):
    tt = x_ref.shape[0]
    t = pl.program_id(1)
    n_pairs = N_HEADS // 2
    lanes = 2 * QBLK

    @pl.when(t == 0)
    def _():
        vT_buf[...] = jnp.ones(vT_buf.shape, BF16)
        k_buf[:, 0:WINDOW, :] = jnp.zeros((N_KV, WINDOW, HEAD_DIM), BF16)
        kr = lax.broadcasted_iota(jnp.int32, (KBLK, lanes), 0)
        ln = lax.broadcasted_iota(jnp.int32, (KBLK, lanes), 1)
        ql = ln % QBLK
        absdist = jnp.abs(ql + WINDOW - kr).astype(F32)
        kc, qc = kr // CHUNK, ql // CHUNK
        in_window = (kc >= qc) & (kc <= qc + WINDOW // CHUNK)
        ln1 = lax.broadcasted_iota(jnp.int32, (SUBLANES, lanes), 1)
        for pr in range(n_pairs):
            s0 = 2.0 ** (-8.0 * (2 * pr + 1) / N_HEADS)
            s1 = 2.0 ** (-8.0 * (2 * pr + 2) / N_HEADS)
            slope = jnp.where(ln < QBLK, s0, s1)
            bias_buf[pr] = jnp.where(in_window, -slope * absdist, NEG_INF)
            sink_buf[pr] = jnp.where(ln1 < QBLK, sink_ref[2 * pr], sink_ref[2 * pr + 1])

    x = x_ref[...]
    xb = x.astype(BF16)
    nt_dims = (((1,), (1,)), ((), ()))
    qT = lax.dot_general(wqT_ref[...], xb, nt_dims, preferred_element_type=F32)
    qT_buf[...] = (qT * (HEAD_DIM ** -0.5)).astype(BF16)
    k = _dot(xb, wk_ref[...])
    kb = k.astype(BF16)
    vT = lax.dot_general(wvT_ref[...], xb, nt_dims, preferred_element_type=F32)
    for kh in range(N_KV):
        k_buf[kh, WINDOW:WINDOW + tt, :] = kb[:, kh * HEAD_DIM:(kh + 1) * HEAD_DIM]
        vT_buf[kh, 0:HEAD_DIM, WINDOW:WINDOW + tt] = vT[kh * HEAD_DIM:(kh + 1) * HEAD_DIM, :].astype(BF16)
    k_out_ref[...] = k[tt - WINDOW:tt, :]

    @pl.when(t == pl.num_programs(1) - 1)
    def _():
        v_out_ref[...] = _dot(xb[tt - WINDOW:tt, :], wv_ref[...])

    def scores(qb, pr, key_lo):
        kh = pr // (GROUP // 2)
        q_lanes = slice(qb * QBLK, (qb + 1) * QBLK)
        h_rows = [slice((2 * pr + i) * HEAD_DIM, (2 * pr + i + 1) * HEAD_DIM) for i in range(2)]
        q2 = jnp.concatenate([qT_buf[h_rows[0], q_lanes], qT_buf[h_rows[1], q_lanes]], axis=1)
        keys = slice(qb * QBLK + key_lo, qb * QBLK + KBLK)
        return _dot(k_buf[kh, keys, :], q2) + bias_buf[pr, key_lo:KBLK, :]

    def finish(qb, pr, key_lo, sT):
        kh = pr // (GROUP // 2)
        q_lanes = slice(qb * QBLK, (qb + 1) * QBLK)
        h_rows = [slice((2 * pr + i) * HEAD_DIM, (2 * pr + i + 1) * HEAD_DIM) for i in range(2)]
        keys = slice(qb * QBLK + key_lo, qb * QBLK + KBLK)
        sink = sink_buf[pr][0:1, :]
        m = jnp.maximum(jnp.max(sT, axis=0, keepdims=True), sink)
        p = jnp.exp(sT - m)
        pv = _dot(vT_buf[kh, :, keys], p.astype(BF16))
        den = pv[HEAD_DIM:HEAD_DIM + 1, :] + jnp.exp(sink - m)
        oT = (pv[0:HEAD_DIM, :] * (1.0 / den)).astype(BF16)
        oT_buf[h_rows[0], q_lanes] = oT[:, 0:QBLK]
        oT_buf[h_rows[1], q_lanes] = oT[:, QBLK:lanes]

    def attend_all(first_key_lo):
        items = [(qb, pr, first_key_lo if qb == 0 else 0) for qb in range(tt // QBLK) for pr in range(n_pairs)]
        pending = []
        for item in items:
            pending.append((item, scores(*item)))
            if len(pending) > ATTN_SKEW:
                done, sT = pending.pop(0)
                finish(*done, sT)
        for done, sT in pending:
            finish(*done, sT)

    @pl.when(t == 0)
    def _():
        attend_all(WINDOW)

    @pl.when(t != 0)
    def _():
        attend_all(0)

    out = lax.dot_general(oT_buf[...], w_out_ref[...], (((0,), (0,)), ((), ())), preferred_element_type=F32)
    y_ref[...] = _layer_norm(alpha * x + out, lng_ref[...], lnb_ref[...])
    k_buf[:, 0:WINDOW, :] = k_buf[:, tt:tt + WINDOW, :]
    vT_buf[:, 0:HEAD_DIM, 0:WINDOW] = vT_buf[:, 0:HEAD_DIM, tt:tt + WINDOW]


def _mixer_c_prompt_call(x, wqT, wk, wv, wvT, sinks, w_out, ln_g, ln_b, alpha, tt):
    bsz, seq, d_model = x.shape
    d_q = N_HEADS * HEAD_DIM
    d_kv = N_KV * HEAD_DIM
    assert seq % tt == 0 and tt % QBLK == 0 and tt >= WINDOW
    consts = (wqT, wk, wv, wvT, w_out, ln_g, ln_b)
    in_specs = [
        pl.BlockSpec(memory_space=pltpu.SMEM),
        pl.BlockSpec((None, tt, d_model), lambda b, t: (b, t, 0)),
    ] + [_const_spec(c) for c in consts]
    out_shape = (
        jax.ShapeDtypeStruct((bsz, seq, d_model), F32),
        jax.ShapeDtypeStruct((bsz, WINDOW, d_kv), F32),
        jax.ShapeDtypeStruct((bsz, WINDOW, d_kv), F32),
    )
    out_specs = (
        pl.BlockSpec((None, tt, d_model), lambda b, t: (b, t, 0)),
        pl.BlockSpec((None, WINDOW, d_kv), lambda b, t: (b, 0, 0)),
        pl.BlockSpec((None, WINDOW, d_kv), lambda b, t: (b, 0, 0)),
    )
    scratch = [
        pltpu.VMEM((d_q, tt), BF16),
        pltpu.VMEM((N_KV, tt + WINDOW, HEAD_DIM), BF16),
        pltpu.VMEM((N_KV, HEAD_DIM + ONES_ROWS, tt + WINDOW), BF16),
        pltpu.VMEM((d_q, tt), BF16),
        pltpu.VMEM((N_HEADS // 2, KBLK, 2 * QBLK), F32),
        pltpu.VMEM((N_HEADS // 2, SUBLANES, 2 * QBLK), F32),
    ]
    return pl.pallas_call(
        functools.partial(_mixer_c_prompt_body, alpha),
        grid=(bsz, seq // tt), in_specs=in_specs, out_specs=out_specs, out_shape=out_shape,
        scratch_shapes=scratch,
        compiler_params=pltpu.CompilerParams(
            dimension_semantics=("arbitrary", "arbitrary"), vmem_limit_bytes=VMEM_LIMIT_BYTES),
        name="mixer_c_prompt",
    )(sinks, x, *consts)


FFN_UNROLL = 5


def _ffn_body(alpha, x_ref, wg_ref, wu_ref, wd_ref, lng_ref, lnb_ref, y_ref, xb_buf, acc_buf):
    n_chunks = wg_ref.shape[0]
    xb_buf[...] = x_ref[...].astype(BF16)

    def down(c):
        xb = xb_buf[...]
        g = _dot(xb, wg_ref[c])
        u = _dot(xb, wu_ref[c])
        hid = (g * _sigmoid(g) * u).astype(BF16)
        return _dot(hid, wd_ref[c])

    acc_buf[...] = alpha * x_ref[...] + down(0)

    def chunk_body(c, carry):
        acc_buf[...] += down(c)
        return carry

    lax.fori_loop(1, n_chunks, chunk_body, 0, unroll=FFN_UNROLL)
    y_ref[...] = _layer_norm(acc_buf[...], lng_ref[...], lnb_ref[...])


def _ffn_call(x2d, layer, wg, wu, wd, ln_g, ln_b, alpha, tm):
    rows, d_model = x2d.shape
    assert rows % tm == 0
    consts = (wg, wu, wd, ln_g, ln_b)

    def const_spec(arr):
        zeros = (0,) * (arr.ndim - 1)
        return pl.BlockSpec((None,) + arr.shape[1:], lambda i: (layer,) + zeros, pipeline_mode=pl.Buffered(1))

    return pl.pallas_call(
        functools.partial(_ffn_body, alpha),
        grid=(rows // tm,),
        in_specs=[pl.BlockSpec((tm, d_model), lambda i: (i, 0))] + [const_spec(c) for c in consts],
        out_specs=pl.BlockSpec((tm, d_model), lambda i: (i, 0)),
        out_shape=jax.ShapeDtypeStruct((rows, d_model), F32),
        scratch_shapes=[pltpu.VMEM((tm, d_model), BF16), pltpu.VMEM((tm, d_model), F32)],
        compiler_params=pltpu.CompilerParams(
            dimension_semantics=("arbitrary",), vmem_limit_bytes=VMEM_LIMIT_BYTES),
        name="ffn",
    )(x2d, *consts)


def _ffn(x, layer, ffn_prm, alpha, tm):
    d_model = x.shape[-1]
    return _ffn_call(x.reshape(-1, d_model), layer, *ffn_prm, alpha, tm).reshape(x.shape)


def _row(v):
    return v.reshape(1, -1).astype(F32)


def _gate_tiles(ga_w, gx_w):
    heads, blk, _ = ga_w.shape
    per_tile = MXU_DIM // blk
    tiles = []
    for j in range(heads // per_tile):
        def diag(w):
            return jax.scipy.linalg.block_diag(*[w[j * per_tile + i] for i in range(per_tile)])
        tiles.append(jnp.concatenate([diag(ga_w), diag(gx_w)], axis=1))
    return jnp.stack(tiles).astype(BF16)


def _ffn_chunks(wg, wu, wd):
    depth, d_model, d_ff = wg.shape
    n = d_ff // MXU_DIM
    wg_c = wg.astype(BF16).reshape(depth, d_model, n, MXU_DIM).transpose(0, 2, 1, 3)
    wu_c = wu.astype(BF16).reshape(depth, d_model, n, MXU_DIM).transpose(0, 2, 1, 3)
    wd_c = wd.astype(BF16).reshape(depth, n, MXU_DIM, d_model)
    return wg_c, wu_c, wd_c


def kernel(x_prompt, x_sample, state_rec_h, state_rec_conv, state_cf_conv, cache_k, cache_v, w_in_ab, rec_conv_w, rec_conv_b, rec_gate_a_w, rec_gate_a_b, rec_gate_x_w, rec_gate_x_b, rec_lambda, cf_conv_w, cf_conv_b, cf_norm_g, cf_norm_b, w_out_ab, w_qkv, attn_sinks, w_out_c, ln_mix_g, ln_mix_b, w_ff_gate, w_ff_up, w_ff_down, ln_ff_g, ln_ff_b):
    depth = ln_mix_g.shape[0]
    alpha = (2 * depth) ** 0.25
    bp, seq_p, d_model = x_prompt.shape
    bs, seq_s, _ = x_sample.shape
    d_rec = rec_conv_w.shape[-1]
    d_conv = cf_conv_w.shape[-1]
    d_kv = N_KV * HEAD_DIM
    tt_ab = min(seq_p, 64)
    tt_c = min(seq_p, 1024)
    tt_s = seq_s
    tm_p = min(bp * seq_p, 1024)
    tm_s = bs * seq_s

    ffn_prm = _ffn_chunks(w_ff_gate, w_ff_up, w_ff_down) + (
        ln_ff_g.reshape(depth, 1, d_model).astype(F32), ln_ff_b.reshape(depth, 1, d_model).astype(F32))

    yp, ys = x_prompt, x_sample
    p_h, s_h, p_rc, s_rc, p_cf, s_cf, p_k, s_k, p_v, s_v = ([] for _ in range(10))
    for layer in range(depth):
        j = layer // 2
        ln_g, ln_b = _row(ln_mix_g[layer]), _row(ln_mix_b[layer])
        if layer % 2 == 0:
            prm =(w_in_ab[j].astype(BF16), rec_conv_w[j], _row(rec_conv_b[j]),
                   _gate_tiles(rec_gate_a_w[j], rec_gate_x_w[j]), _row(rec_gate_a_b[j]), _row(rec_gate_x_b[j]),
                   _row(rec_lambda[j]), cf_conv_w[j], _row(cf_conv_b[j]), _row(cf_norm_g[j]), _row(cf_norm_b[j]),
                   w_out_ab[j].astype(BF16))
            yp, rc, hl, cf = _mixer_ab_call(
                yp, jnp.zeros((bp, REC_CONV - 1, d_rec), F32), jnp.zeros((bp, d_rec), F32),
                jnp.zeros((bp, CONV_WIDTH - 1, d_conv), F32), prm, ln_g, ln_b, alpha, tt_ab)
            p_rc.append(rc), p_h.append(hl), p_cf.append(cf)
            ys, rc, hl, cf = _mixer_ab_call(
                ys, state_rec_conv[j], state_rec_h[j], state_cf_conv[j], prm, ln_g, ln_b, alpha, tt_s)
            s_rc.append(rc), s_h.append(hl), s_cf.append(cf)
        else:
            wqkv, w_out = w_qkv[j].astype(BF16), w_out_c[j].astype(BF16)
            sinks = attn_sinks[j].astype(F32)
            d_q = N_HEADS * HEAD_DIM
            wq_t = wqkv[:, 0:d_q].T
            wk, wv = wqkv[:, d_q:d_q + d_kv], wqkv[:, d_q + d_kv:d_q + 2 * d_kv]
            yp, kk, vv = _mixer_c_prompt_call(yp, wq_t, wk, wv, wv.T, sinks, w_out, ln_g, ln_b, alpha, tt_c)
            p_k.append(kk.reshape(bp, WINDOW, N_KV, HEAD_DIM)), p_v.append(vv.reshape(bp, WINDOW, N_KV, HEAD_DIM))
            ck = cache_k[j].reshape(bs, WINDOW, d_kv)
            cv = cache_v[j].reshape(bs, WINDOW, d_kv)
            ys, kk, vv = _mixer_c_call(ys, ck, cv, wqkv, sinks, w_out, ln_g, ln_b, alpha)
            s_k.append(kk.reshape(bs, WINDOW, N_KV, HEAD_DIM)), s_v.append(vv.reshape(bs, WINDOW, N_KV, HEAD_DIM))
        yp = _ffn(yp, layer, ffn_prm, alpha, tm_p)
        ys = _ffn(ys, layer, ffn_prm, alpha, tm_s)
    return (yp, ys, jnp.stack(p_h), jnp.stack(s_h), jnp.stack(p_rc), jnp.stack(s_rc),
            jnp.stack(p_cf), jnp.stack(s_cf), jnp.stack(p_k), jnp.stack(s_k), jnp.stack(p_v), jnp.stack(s_v))
```

```python
import functools

import jax
import jax.numpy as jnp
from jax import lax
from jax.experimental import pallas as pl
from jax.experimental.pallas import tpu as pltpu

CHUNK = 64
WINDOW = 128
HEAD_DIM = 64
N_HEADS = 16
N_KV = 4
GROUP = N_HEADS // N_KV
REC_CONV = 4
CONV_WIDTH = 31
LRU_C = 8.0
LN_EPS = 1e-5
NEG_INF = -1e30

SUBLANES = 8
MXU_DIM = 256
CONV_ROWS = 16
VMEM_LIMIT_BYTES = 56 * 1024 * 1024

BF16 = jnp.bfloat16
F32 = jnp.float32


def _dot(a, b):
    return jnp.dot(a, b, preferred_element_type=F32)


def _layer_norm(z, g, b):
    mu = jnp.mean(z, axis=-1, keepdims=True)
    zc = z - mu
    var = jnp.mean(zc * zc, axis=-1, keepdims=True)
    return zc * lax.rsqrt(var + LN_EPS) * g + b


def _sigmoid(x):
    return jax.nn.sigmoid(x)


def _mixer_ab_body(alpha, x_ref, rc0_ref, h0_ref, cf0_ref, w_in_ref, rcw_ref, rcb_ref, wgate_ref,
                   gab_ref, gxb_ref, lam_ref, cfw_ref, cfb_ref, cfg_ref, cfbeta_ref, w_out_ref,
                   lng_ref, lnb_ref,
                   y_ref, rc_out_ref, h_out_ref, cf_out_ref,
                   xr_buf, g_buf, a_buf, b_buf, h_buf, hcar, c_buf, wb_buf):
    bsz, tt, d_model = x_ref.shape
    rows = bsz * tt
    d_rec = rcw_ref.shape[1]
    d_conv = cfw_ref.shape[1]
    rc_rows = (REC_CONV - 1) * bsz
    cf_rows = (CONV_WIDTH - 1) * bsz
    step = pl.program_id(0)

    @pl.when(step == 0)
    def _():
        xr_buf[0:rc_rows, :] = rc0_ref[...].reshape(rc_rows, d_rec)
        g_buf[0:cf_rows, :] = cf0_ref[...].reshape(cf_rows, d_conv)
        hcar[...] = h0_ref[...]
        for k in range(CONV_WIDTH):
            wb_buf[k] = jnp.broadcast_to(cfw_ref[k:k + 1, :], (SUBLANES, d_conv))

    x = pltpu.einshape("btd->tbd", x_ref[...]).reshape(rows, d_model)
    xb = x.astype(BF16)

    xr = _dot(xb, w_in_ref[:, 0:d_rec])
    yr = _dot(xb, w_in_ref[:, d_rec:2 * d_rec])
    cv = _dot(xb, w_in_ref[:, 2 * d_rec:2 * d_rec + d_conv])
    cg = _dot(xb, w_in_ref[:, 2 * d_rec + d_conv:2 * d_rec + 2 * d_conv])

    xr_buf[rc_rows:rc_rows + rows, :] = xr
    xc = rcb_ref[...] + rcw_ref[REC_CONV - 1:REC_CONV, :] * xr
    for k in range(REC_CONV - 1):
        xc = xc + rcw_ref[k:k + 1, :] * xr_buf[k * bsz:k * bsz + rows, :]
    xcb = xc.astype(BF16)
    r_parts, i_parts = [], []
    for j in range(d_rec // MXU_DIM):
        lo, hi = j * MXU_DIM, (j + 1) * MXU_DIM
        gj = _dot(xcb[:, lo:hi], wgate_ref[j])
        r_parts.append(_sigmoid(gj[:, :MXU_DIM] + gab_ref[:, lo:hi]))
        i_parts.append(_sigmoid(gj[:, MXU_DIM:] + gxb_ref[:, lo:hi]))
    r = jnp.concatenate(r_parts, axis=1)
    i = jnp.concatenate(i_parts, axis=1)
    nlam = -lam_ref[...]
    softplus = jnp.maximum(nlam, 0.0) + jnp.log1p(jnp.exp(-jnp.abs(nlam)))
    log_a = (-LRU_C * softplus) * r
    a = jnp.exp(log_a)
    y = 1.0 - a * a
    a_buf[...] = a
    b_buf[...] = jnp.where(y > 0.0, y * lax.rsqrt(y), 0.0) * (i * xc)

    h = hcar[...]
    for t in range(tt):
        grp = slice(t * bsz, (t + 1) * bsz)
        h = a_buf[grp, :] * h + b_buf[grp, :]
        h_buf[grp, :] = h
    hcar[...] = h

    rec_out = h_buf[...] * jax.nn.gelu(yr)

    g_buf[cf_rows:cf_rows + rows, :] = cv * _sigmoid(cg)
    blk_groups = CONV_ROWS // SUBLANES

    def conv_block(rb, carry):
        row0 = pl.multiple_of(rb * CONV_ROWS, CONV_ROWS)
        acc = jnp.broadcast_to(cfb_ref[...], (CONV_ROWS, d_conv)).reshape(blk_groups, SUBLANES, d_conv)
        for k in range(CONV_WIDTH):
            src = g_buf[pl.ds(row0 + k * bsz, CONV_ROWS), :]
            acc = acc + wb_buf[k] * src.reshape(blk_groups, SUBLANES, d_conv)
        c_buf[pl.ds(row0, CONV_ROWS), :] = acc.reshape(CONV_ROWS, d_conv)
        return carry

    lax.fori_loop(0, rows // CONV_ROWS, conv_block, 0, unroll=True)
    cn = _layer_norm(c_buf[...], cfg_ref[...], cfbeta_ref[...])
    c2 = cn * _sigmoid(cn)

    out = _dot(rec_out.astype(BF16), w_out_ref[0:d_rec, :]) + _dot(c2.astype(BF16), w_out_ref[d_rec:d_rec + d_conv, :])
    y_tile = _layer_norm(alpha * x + out, lng_ref[...], lnb_ref[...])
    y_ref[...] = pltpu.einshape("tbd->btd", y_tile.reshape(tt, bsz, d_model))

    new_rc = xr_buf[rows:rows + rc_rows, :]
    new_cf = g_buf[rows:rows + cf_rows, :]
    xr_buf[0:rc_rows, :] = new_rc
    g_buf[0:cf_rows, :] = new_cf
    rc_out_ref[...] = new_rc.reshape(REC_CONV - 1, bsz, d_rec)
    cf_out_ref[...] = new_cf.reshape(CONV_WIDTH - 1, bsz, d_conv)
    h_out_ref[...] = h


def _const_spec(arr):
    zeros = (0,) * arr.ndim
    return pl.BlockSpec(arr.shape, lambda b, t: zeros)


def _mixer_ab_call(x, rc0, h0, cf0, prm, ln_g, ln_b, alpha, tt):
    bsz, seq, d_model = x.shape
    (w_in, rcw, rcb, wgate, gab, gxb, lam, cfw, cfb, cfg, cfbeta, w_out) = prm
    d_rec, d_conv = rcw.shape[1], cfw.shape[1]
    assert bsz == SUBLANES and seq % tt == 0 and (bsz * tt) % CONV_ROWS == 0
    consts = (w_in, rcw, rcb, wgate, gab, gxb, lam, cfw, cfb, cfg, cfbeta, w_out, ln_g, ln_b)
    states = (jnp.swapaxes(rc0, 0, 1), h0, jnp.swapaxes(cf0, 0, 1))

    def whole(arr):
        zeros = (0,) * arr.ndim
        return pl.BlockSpec(arr.shape, lambda s: zeros)

    state_shapes = (
        jax.ShapeDtypeStruct((REC_CONV - 1, bsz, d_rec), F32),
        jax.ShapeDtypeStruct((bsz, d_rec), F32),
        jax.ShapeDtypeStruct((CONV_WIDTH - 1, bsz, d_conv), F32),
    )
    rows = bsz * tt
    scratch = [
        pltpu.VMEM(((REC_CONV - 1) * bsz + rows, d_rec), F32),
        pltpu.VMEM(((CONV_WIDTH - 1) * bsz + rows, d_conv), F32),
        pltpu.VMEM((rows, d_rec), F32),
        pltpu.VMEM((rows, d_rec), F32),
        pltpu.VMEM((rows, d_rec), F32),
        pltpu.VMEM((bsz, d_rec), F32),
        pltpu.VMEM((rows, d_conv), F32),
        pltpu.VMEM((CONV_WIDTH, SUBLANES, d_conv), F32),
    ]
    y, rc, h, cf = pl.pallas_call(
        functools.partial(_mixer_ab_body, alpha),
        grid=(seq // tt,),
        in_specs=[pl.BlockSpec((bsz, tt, d_model), lambda s: (0, s, 0))]
        + [whole(a) for a in states] + [whole(c) for c in consts],
        out_specs=(pl.BlockSpec((bsz, tt, d_model), lambda s: (0, s, 0)),) + tuple(whole(a) for a in state_shapes),
        out_shape=(jax.ShapeDtypeStruct((bsz, seq, d_model), F32),) + state_shapes,
        scratch_shapes=scratch,
        compiler_params=pltpu.CompilerParams(
            dimension_semantics=("arbitrary",), vmem_limit_bytes=VMEM_LIMIT_BYTES),
        name="mixer_ab",
    )(x, *states, *consts)
    return y, jnp.swapaxes(rc, 0, 1), h, jnp.swapaxes(cf, 0, 1)


def _mixer_c_body(alpha, sink_ref, x_ref, k0_ref, v0_ref, wqkv_ref, w_out_ref, lng_ref, lnb_ref,
                  y_ref, k_out_ref, v_out_ref,
                  q_buf, k_buf, v_buf, o_buf, bias_buf, sink_buf):
    bsz, tt, d_model = x_ref.shape
    rows = bsz * tt
    d_q = N_HEADS * HEAD_DIM
    d_kv = N_KV * HEAD_DIM
    win_keys = WINDOW + CHUNK
    stack = GROUP * CHUNK

    row = lax.broadcasted_iota(jnp.int32, (stack, win_keys), 0)
    ko = lax.broadcasted_iota(jnp.int32, (stack, win_keys), 1)
    absdist = jnp.abs(WINDOW + (row % CHUNK) - ko).astype(F32)
    row1 = lax.broadcasted_iota(jnp.int32, (stack, 1), 0)
    for kh in range(N_KV):
        slope = jnp.zeros((stack, win_keys), F32)
        sink = jnp.zeros((stack, 1), F32)
        for g in range(GROUP):
            h = kh * GROUP + g
            slope = jnp.where(row // CHUNK == g, 2.0 ** (-8.0 * (h + 1) / N_HEADS), slope)
            sink = jnp.where(row1 // CHUNK == g, sink_ref[h], sink)
        bias_buf[kh] = -slope * absdist
        sink_buf[kh] = sink

    x = x_ref[...].reshape(rows, d_model)
    xb = x.astype(BF16)
    q_buf[...] = (_dot(xb, wqkv_ref[:, 0:d_q]) * (HEAD_DIM ** -0.5)).astype(BF16)
    k = _dot(xb, wqkv_ref[:, d_q:d_q + d_kv]).reshape(bsz, tt, d_kv)
    v = _dot(xb, wqkv_ref[:, d_q + d_kv:d_q + 2 * d_kv]).reshape(bsz, tt, d_kv)
    k_buf[:, 0:WINDOW, :] = k0_ref[...].astype(BF16)
    v_buf[:, 0:WINDOW, :] = v0_ref[...].astype(BF16)
    k_buf[:, WINDOW:win_keys, :] = k.astype(BF16)
    v_buf[:, WINDOW:win_keys, :] = v.astype(BF16)
    k_out_ref[:, 0:WINDOW - tt, :] = k0_ref[:, tt:WINDOW, :]
    v_out_ref[:, 0:WINDOW - tt, :] = v0_ref[:, tt:WINDOW, :]
    k_out_ref[:, WINDOW - tt:WINDOW, :] = k
    v_out_ref[:, WINDOW - tt:WINDOW, :] = v

    def seq_body(b, carry):
        r0 = pl.multiple_of(b * CHUNK, CHUNK)
        scores = []
        for kh in range(N_KV):
            q4 = q_buf[pl.ds(r0, CHUNK), kh * stack:(kh + 1) * stack]
            qs = jnp.concatenate([q4[:, g * HEAD_DIM:(g + 1) * HEAD_DIM] for g in range(GROUP)], axis=0)
            kwh = k_buf[b, :, kh * HEAD_DIM:(kh + 1) * HEAD_DIM]
            s = lax.dot_general(qs, kwh, (((1,), (1,)), ((), ())), preferred_element_type=F32)
            scores.append(s + bias_buf[kh])
        probs, denoms = [], []
        for kh in range(N_KV):
            s = scores[kh]
            sink = sink_buf[kh]
            m = jnp.maximum(jnp.max(s, axis=-1, keepdims=True), sink)
            p = jnp.exp(s - m)
            denoms.append(jnp.sum(p, axis=-1, keepdims=True) + jnp.exp(sink - m))
            probs.append(p.astype(BF16))
        for kh in range(N_KV):
            vwh = v_buf[b, :, kh * HEAD_DIM:(kh + 1) * HEAD_DIM]
            o = (_dot(probs[kh], vwh) / denoms[kh]).astype(BF16)
            for g in range(GROUP):
                h = kh * GROUP + g
                o_buf[pl.ds(r0, CHUNK), h * HEAD_DIM:(h + 1) * HEAD_DIM] = o[g * CHUNK:(g + 1) * CHUNK, :]
        return carry

    lax.fori_loop(0, bsz, seq_body, 0)

    out = _dot(o_buf[...], w_out_ref[...])
    y_ref[...] = _layer_norm(alpha * x + out, lng_ref[...], lnb_ref[...]).reshape(bsz, tt, d_model)


def _mixer_c_call(x, k0, v0, wqkv, sinks, w_out, ln_g, ln_b, alpha):
    bsz, seq, d_model = x.shape
    d_q = N_HEADS * HEAD_DIM
    d_kv = N_KV * HEAD_DIM
    assert seq == CHUNK and k0.shape == (bsz, WINDOW, d_kv)
    operands = (x, k0, v0, wqkv, w_out, ln_g, ln_b)

    def whole(arr):
        zeros = (0,) * arr.ndim
        return pl.BlockSpec(arr.shape, lambda s: zeros)

    out_shape = (
        jax.ShapeDtypeStruct((bsz, seq, d_model), F32),
        jax.ShapeDtypeStruct((bsz, WINDOW, d_kv), F32),
        jax.ShapeDtypeStruct((bsz, WINDOW, d_kv), F32),
    )
    scratch = [
        pltpu.VMEM((bsz * seq, d_q), BF16),
        pltpu.VMEM((bsz, WINDOW + CHUNK, d_kv), BF16),
        pltpu.VMEM((bsz, WINDOW + CHUNK, d_kv), BF16),
        pltpu.VMEM((bsz * seq, d_q), BF16),
        pltpu.VMEM((N_KV, GROUP * CHUNK, WINDOW + CHUNK), F32),
        pltpu.VMEM((N_KV, GROUP * CHUNK, 1), F32),
    ]
    return pl.pallas_call(
        functools.partial(_mixer_c_body, alpha),
        grid=(1,),
        in_specs=[pl.BlockSpec(memory_space=pltpu.SMEM)] + [whole(a) for a in operands],
        out_specs=tuple(whole(a) for a in out_shape), out_shape=out_shape,
        scratch_shapes=scratch,
        compiler_params=pltpu.CompilerParams(
            dimension_semantics=("arbitrary",), vmem_limit_bytes=VMEM_LIMIT_BYTES),
        name="mixer_c",
    )(sinks, *operands)


QBLK = 2 * CHUNK
KBLK = WINDOW + QBLK
ONES_ROWS = 16
ATTN_SKEW = 4


def _mixer_c_prompt_body(alpha, sink_ref, x_ref, wqT_ref, wk_ref, wv_ref, wvT_ref, w_out_ref, lng_ref, lnb_ref,
                         y_ref, k_out_ref, v_out_ref,
                         qT_buf, k_buf, vT_buf, oT_buf, bias_buf, sink_buf):
    tt = x_ref.shape[0]
    t = pl.program_id(1)
    n_pairs = N_HEADS // 2
    lanes = 2 * QBLK

    @pl.when(t == 0)
    def _():
        vT_buf[...] = jnp.ones(vT_buf.shape, BF16)
        k_buf[:, 0:WINDOW, :] = jnp.zeros((N_KV, WINDOW, HEAD_DIM), BF16)
        kr = lax.broadcasted_iota(jnp.int32, (KBLK, lanes), 0)
        ln = lax.broadcasted_iota(jnp.int32, (KBLK, lanes), 1)
        ql = ln % QBLK
        absdist = jnp.abs(ql + WINDOW - kr).astype(F32)
        kc, qc = kr // CHUNK, ql // CHUNK
        in_window = (kc >= qc) & (kc <= qc + WINDOW // CHUNK)
        ln1 = lax.broadcasted_iota(jnp.int32, (SUBLANES, lanes), 1)
        for pr in range(n_pairs):
            s0 = 2.0 ** (-8.0 * (2 * pr + 1) / N_HEADS)
            s1 = 2.0 ** (-8.0 * (2 * pr + 2) / N_HEADS)
            slope = jnp.where(ln < QBLK, s0, s1)
            bias_buf[pr] = jnp.where(in_window, -slope * absdist, NEG_INF)
            sink_buf[pr] = jnp.where(ln1 < QBLK, sink_ref[2 * pr], sink_ref[2 * pr + 1])

    x = x_ref[...]
    xb = x.astype(BF16)
    nt_dims = (((1,), (1,)), ((), ()))
    qT = lax.dot_general(wqT_ref[...], xb, nt_dims, preferred_element_type=F32)
    qT_buf[...] = (qT * (HEAD_DIM ** -0.5)).astype(BF16)
    k = _dot(xb, wk_ref[...])
    kb = k.astype(BF16)
    vT = lax.dot_general(wvT_ref[...], xb, nt_dims, preferred_element_type=F32)
    for kh in range(N_KV):
        k_buf[kh, WINDOW:WINDOW + tt, :] = kb[:, kh * HEAD_DIM:(kh + 1) * HEAD_DIM]
        vT_buf[kh, 0:HEAD_DIM, WINDOW:WINDOW + tt] = vT[kh * HEAD_DIM:(kh + 1) * HEAD_DIM, :].astype(BF16)
    k_out_ref[...] = k[tt - WINDOW:tt, :]

    @pl.when(t == pl.num_programs(1) - 1)
    def _():
        v_out_ref[...] = _dot(xb[tt - WINDOW:tt, :], wv_ref[...])

    def scores(qb, pr, key_lo):
        kh = pr // (GROUP // 2)
        q_lanes = slice(qb * QBLK, (qb + 1) * QBLK)
        h_rows = [slice((2 * pr + i) * HEAD_DIM, (2 * pr + i + 1) * HEAD_DIM) for i in range(2)]
        q2 = jnp.concatenate([qT_buf[h_rows[0], q_lanes], qT_buf[h_rows[1], q_lanes]], axis=1)
        keys = slice(qb * QBLK + key_lo, qb * QBLK + KBLK)
        return _dot(k_buf[kh, keys, :], q2) + bias_buf[pr, key_lo:KBLK, :]

    def finish(qb, pr, key_lo, sT):
        kh = pr // (GROUP // 2)
        q_lanes = slice(qb * QBLK, (qb + 1) * QBLK)
        h_rows = [slice((2 * pr + i) * HEAD_DIM, (2 * pr + i + 1) * HEAD_DIM) for i in range(2)]
        keys = slice(qb * QBLK + key_lo, qb * QBLK + KBLK)
        sink = sink_buf[pr][0:1, :]
        m = jnp.maximum(jnp.max(sT, axis=0, keepdims=True), sink)
        p = jnp.exp(sT - m)
        pv = _dot(vT_buf[kh, :, keys], p.astype(BF16))
        den = pv[HEAD_DIM:HEAD_DIM + 1, :] + jnp.exp(sink - m)
        oT = (pv[0:HEAD_DIM, :] * (1.0 / den)).astype(BF16)
        oT_buf[h_rows[0], q_lanes] = oT[:, 0:QBLK]
        oT_buf[h_rows[1], q_lanes] = oT[:, QBLK:lanes]

    def attend_all(first_key_lo):
        items = [(qb, pr, first_key_lo if qb == 0 else 0) for qb in range(tt // QBLK) for pr in range(n_pairs)]
        pending = []
        for item in items:
            pending.append((item, scores(*item)))
            if len(pending) > ATTN_SKEW:
                done, sT = pending.pop(0)
                finish(*done, sT)
        for done, sT in pending:
            finish(*done, sT)

    @pl.when(t == 0)
    def _():
        attend_all(WINDOW)

    @pl.when(t != 0)
    def _():
        attend_all(0)

    out = lax.dot_general(oT_buf[...], w_out_ref[...], (((0,), (0,)), ((), ())), preferred_element_type=F32)
    y_ref[...] = _layer_norm(alpha * x + out, lng_ref[...], lnb_ref[...])
    k_buf[:, 0:WINDOW, :] = k_buf[:, tt:tt + WINDOW, :]
    vT_buf[:, 0:HEAD_DIM, 0:WINDOW] = vT_buf[:, 0:HEAD_DIM, tt:tt + WINDOW]


def _mixer_c_prompt_call(x, wqT, wk, wv, wvT, sinks, w_out, ln_g, ln_b, alpha, tt):
    bsz, seq, d_model = x.shape
    d_q = N_HEADS * HEAD_DIM
    d_kv = N_KV * HEAD_DIM
    assert seq % tt == 0 and tt % QBLK == 0 and tt >= WINDOW
    consts = (wqT, wk, wv, wvT, w_out, ln_g, ln_b)
    in_specs = [
        pl.BlockSpec(memory_space=pltpu.SMEM),
        pl.BlockSpec((None, tt, d_model), lambda b, t: (b, t, 0)),
    ] + [_const_spec(c) for c in consts]
    out_shape = (
        jax.ShapeDtypeStruct((bsz, seq, d_model), F32),
        jax.ShapeDtypeStruct((bsz, WINDOW, d_kv), F32),
        jax.ShapeDtypeStruct((bsz, WINDOW, d_kv), F32),
    )
    out_specs = (
        pl.BlockSpec((None, tt, d_model), lambda b, t: (b, t, 0)),
        pl.BlockSpec((None, WINDOW, d_kv), lambda b, t: (b, 0, 0)),
        pl.BlockSpec((None, WINDOW, d_kv), lambda b, t: (b, 0, 0)),
    )
    scratch = [
        pltpu.VMEM((d_q, tt), BF16),
        pltpu.VMEM((N_KV, tt + WINDOW, HEAD_DIM), BF16),
        pltpu.VMEM((N_KV, HEAD_DIM + ONES_ROWS, tt + WINDOW), BF16),
        pltpu.VMEM((d_q, tt), BF16),
        pltpu.VMEM((N_HEADS // 2, KBLK, 2 * QBLK), F32),
        pltpu.VMEM((N_HEADS // 2, SUBLANES, 2 * QBLK), F32),
    ]
    return pl.pallas_call(
        functools.partial(_mixer_c_prompt_body, alpha),
        grid=(bsz, seq // tt), in_specs=in_specs, out_specs=out_specs, out_shape=out_shape,
        scratch_shapes=scratch,
        compiler_params=pltpu.CompilerParams(
            dimension_semantics=("arbitrary", "arbitrary"), vmem_limit_bytes=VMEM_LIMIT_BYTES),
        name="mixer_c_prompt",
    )(sinks, x, *consts)


FFN_CHUNK = MXU_DIM


def _ffn_body(alpha, x_ref, wg_ref, wu_ref, wd_ref, lng_ref, lnb_ref, y_ref, xb_buf, acc_buf):
    n_chunks = wg_ref.shape[1] // FFN_CHUNK
    xb_buf[...] = x_ref[...].astype(BF16)

    def down(c):
        cols = slice(c * FFN_CHUNK, (c + 1) * FFN_CHUNK)
        xb = xb_buf[...]
        g = _dot(xb, wg_ref[:, cols])
        u = _dot(xb, wu_ref[:, cols])
        hid = (g * _sigmoid(g) * u).astype(BF16)
        return _dot(hid, wd_ref[cols, :])

    acc_buf[...] = alpha * x_ref[...] + down(0)
    for c in range(1, n_chunks):
        acc_buf[...] += down(c)
    y_ref[...] = _layer_norm(acc_buf[...], lng_ref[...], lnb_ref[...])


def _ffn_call(x2d, layer, wg, wu, wd, ln_g, ln_b, alpha, tm):
    rows, d_model = x2d.shape
    assert rows % tm == 0 and wg.shape[2] % FFN_CHUNK == 0
    consts = (wg, wu, wd, ln_g, ln_b)

    def const_spec(arr):
        zeros = (0,) * (arr.ndim - 1)
        return pl.BlockSpec((None,) + arr.shape[1:], lambda i: (layer,) + zeros, pipeline_mode=pl.Buffered(1))

    return pl.pallas_call(
        functools.partial(_ffn_body, alpha),
        grid=(rows // tm,),
        in_specs=[pl.BlockSpec((tm, d_model), lambda i: (i, 0))] + [const_spec(c) for c in consts],
        out_specs=pl.BlockSpec((tm, d_model), lambda i: (i, 0)),
        out_shape=jax.ShapeDtypeStruct((rows, d_model), F32),
        scratch_shapes=[pltpu.VMEM((tm, d_model), BF16), pltpu.VMEM((tm, d_model), F32)],
        compiler_params=pltpu.CompilerParams(
            dimension_semantics=("arbitrary",), vmem_limit_bytes=VMEM_LIMIT_BYTES),
        name="ffn",
    )(x2d, *consts)


def _ffn(x, layer, ffn_prm, alpha, tm):
    d_model = x.shape[-1]
    return _ffn_call(x.reshape(-1, d_model), layer, *ffn_prm, alpha, tm).reshape(x.shape)


def _row(v):
    return v.reshape(1, -1).astype(F32)


def _gate_tiles(ga_w, gx_w):
    heads, blk, _ = ga_w.shape
    per_tile = MXU_DIM // blk
    tiles = []
    for j in range(heads // per_tile):
        def diag(w):
            return jax.scipy.linalg.block_diag(*[w[j * per_tile + i] for i in range(per_tile)])
        tiles.append(jnp.concatenate([diag(ga_w), diag(gx_w)], axis=1))
    return jnp.stack(tiles).astype(BF16)


def kernel(x_prompt, x_sample, state_rec_h, state_rec_conv, state_cf_conv, cache_k, cache_v, w_in_ab, rec_conv_w, rec_conv_b, rec_gate_a_w, rec_gate_a_b, rec_gate_x_w, rec_gate_x_b, rec_lambda, cf_conv_w, cf_conv_b, cf_norm_g, cf_norm_b, w_out_ab, w_qkv, attn_sinks, w_out_c, ln_mix_g, ln_mix_b, w_ff_gate, w_ff_up, w_ff_down, ln_ff_g, ln_ff_b):
    depth = ln_mix_g.shape[0]
    alpha = (2 * depth) ** 0.25
    bp, seq_p, d_model = x_prompt.shape
    bs, seq_s, _ = x_sample.shape
    d_rec = rec_conv_w.shape[-1]
    d_conv = cf_conv_w.shape[-1]
    d_kv = N_KV * HEAD_DIM
    tt_ab = min(seq_p, 64)
    tt_c = min(seq_p, 1024)
    tt_s = seq_s
    tm_p = min(bp * seq_p, 1024)
    tm_s = bs * seq_s

    ffn_prm = (w_ff_gate.astype(BF16), w_ff_up.astype(BF16), w_ff_down.astype(BF16),
               ln_ff_g.reshape(depth, 1, d_model).astype(F32), ln_ff_b.reshape(depth, 1, d_model).astype(F32))

    yp, ys = x_prompt, x_sample
    p_h, s_h, p_rc, s_rc, p_cf, s_cf, p_k, s_k, p_v, s_v = ([] for _ in range(10))
    for layer in range(depth):
        j = layer // 2
        ln_g, ln_b = _row(ln_mix_g[layer]), _row(ln_mix_b[layer])
        if layer % 2 == 0:
            prm =(w_in_ab[j].astype(BF16), rec_conv_w[j], _row(rec_conv_b[j]),
                   _gate_tiles(rec_gate_a_w[j], rec_gate_x_w[j]), _row(rec_gate_a_b[j]), _row(rec_gate_x_b[j]),
                   _row(rec_lambda[j]), cf_conv_w[j], _row(cf_conv_b[j]), _row(cf_norm_g[j]), _row(cf_norm_b[j]),
                   w_out_ab[j].astype(BF16))
            yp, rc, hl, cf = _mixer_ab_call(
                yp, jnp.zeros((bp, REC_CONV - 1, d_rec), F32), jnp.zeros((bp, d_rec), F32),
                jnp.zeros((bp, CONV_WIDTH - 1, d_conv), F32), prm, ln_g, ln_b, alpha, tt_ab)
            p_rc.append(rc), p_h.append(hl), p_cf.append(cf)
            ys, rc, hl, cf = _mixer_ab_call(
                ys, state_rec_conv[j], state_rec_h[j], state_cf_conv[j], prm, ln_g, ln_b, alpha, tt_s)
            s_rc.append(rc), s_h.append(hl), s_cf.append(cf)
        else:
            wqkv, w_out = w_qkv[j].astype(BF16), w_out_c[j].astype(BF16)
            sinks = attn_sinks[j].astype(F32)
            d_q = N_HEADS * HEAD_DIM
            wq_t = wqkv[:, 0:d_q].T
            wk, wv = wqkv[:, d_q:d_q + d_kv], wqkv[:, d_q + d_kv:d_q + 2 * d_kv]
            yp, kk, vv = _mixer_c_prompt_call(yp, wq_t, wk, wv, wv.T, sinks, w_out, ln_g, ln_b, alpha, tt_c)
            p_k.append(kk.reshape(bp, WINDOW, N_KV, HEAD_DIM)), p_v.append(vv.reshape(bp, WINDOW, N_KV, HEAD_DIM))
            ck = cache_k[j].reshape(bs, WINDOW, d_kv)
            cv = cache_v[j].reshape(bs, WINDOW, d_kv)
            ys, kk, vv = _mixer_c_call(ys, ck, cv, wqkv, sinks, w_out, ln_g, ln_b, alpha)
            s_k.append(kk.reshape(bs, WINDOW, N_KV, HEAD_DIM)), s_v.append(vv.reshape(bs, WINDOW, N_KV, HEAD_DIM))
        yp = _ffn(yp, layer, ffn_prm, alpha, tm_p)
        ys = _ffn(ys, layer, ffn_prm, alpha, tm_s)
    return (yp, ys, jnp.stack(p_h), jnp.stack(s_h), jnp.stack(p_rc), jnp.stack(s_rc),
            jnp.stack(p_cf), jnp.stack(s_cf), jnp.stack(p_k), jnp.stack(s_k), jnp.stack(p_v), jnp.stack(s_v))
```

```python
import functools

import jax
import jax.numpy as jnp
from jax import lax
from jax.experimental import pallas as pl
from jax.experimental.pallas import tpu as pltpu

CHUNK = 64
WINDOW = 128
HEAD_DIM = 64
N_HEADS = 16
N_KV = 4
GROUP = N_HEADS // N_KV
REC_CONV = 4
CONV_WIDTH = 31
LRU_C = 8.0
LN_EPS = 1e-5
NEG_INF = -1e30

SUBLANES = 8
MXU_DIM = 256
CONV_ROWS = 16
VMEM_LIMIT_BYTES = 56 * 1024 * 1024

BF16 = jnp.bfloat16
F32 = jnp.float32


def _dot(a, b):
    return jnp.dot(a, b, preferred_element_type=F32)


def _layer_norm(z, g, b):
    mu = jnp.mean(z, axis=-1, keepdims=True)
    zc = z - mu
    var = jnp.mean(zc * zc, axis=-1, keepdims=True)
    return zc * lax.rsqrt(var + LN_EPS) * g + b


def _sigmoid(x):
    return jax.nn.sigmoid(x)


def _mixer_ab_body(alpha, x_ref, rc0_ref, h0_ref, cf0_ref, w_in_ref, rcw_ref, rcb_ref, wgate_ref,
                   gab_ref, gxb_ref, lam_ref, cfw_ref, cfb_ref, cfg_ref, cfbeta_ref, w_out_ref,
                   lng_ref, lnb_ref,
                   y_ref, rc_out_ref, h_out_ref, cf_out_ref,
                   xr_buf, g_buf, a_buf, b_buf, h_buf, hcar, c_buf, wb_buf):
    bsz, tt, d_model = x_ref.shape
    rows = bsz * tt
    d_rec = rcw_ref.shape[1]
    d_conv = cfw_ref.shape[1]
    rc_rows = (REC_CONV - 1) * bsz
    cf_rows = (CONV_WIDTH - 1) * bsz
    step = pl.program_id(0)

    @pl.when(step == 0)
    def _():
        xr_buf[0:rc_rows, :] = rc0_ref[...].reshape(rc_rows, d_rec)
        g_buf[0:cf_rows, :] = cf0_ref[...].reshape(cf_rows, d_conv)
        hcar[...] = h0_ref[...]
        for k in range(CONV_WIDTH):
            wb_buf[k] = jnp.broadcast_to(cfw_ref[k:k + 1, :], (SUBLANES, d_conv))

    xb = pltpu.einshape("btd->tbd", x_ref[...].astype(BF16)).reshape(rows, d_model)

    xr = _dot(xb, w_in_ref[:, 0:d_rec])
    yr = _dot(xb, w_in_ref[:, d_rec:2 * d_rec])
    cv = _dot(xb, w_in_ref[:, 2 * d_rec:2 * d_rec + d_conv])
    cg = _dot(xb, w_in_ref[:, 2 * d_rec + d_conv:2 * d_rec + 2 * d_conv])

    xr_buf[rc_rows:rc_rows + rows, :] = xr
    xc = rcb_ref[...] + rcw_ref[REC_CONV - 1:REC_CONV, :] * xr
    for k in range(REC_CONV - 1):
        xc = xc + rcw_ref[k:k + 1, :] * xr_buf[k * bsz:k * bsz + rows, :]
    xcb = xc.astype(BF16)
    r_parts, i_parts = [], []
    for j in range(d_rec // MXU_DIM):
        lo, hi = j * MXU_DIM, (j + 1) * MXU_DIM
        gj = _dot(xcb[:, lo:hi], wgate_ref[j])
        r_parts.append(_sigmoid(gj[:, :MXU_DIM] + gab_ref[:, lo:hi]))
        i_parts.append(_sigmoid(gj[:, MXU_DIM:] + gxb_ref[:, lo:hi]))
    r = jnp.concatenate(r_parts, axis=1)
    i = jnp.concatenate(i_parts, axis=1)
    nlam = -lam_ref[...]
    softplus = jnp.maximum(nlam, 0.0) + jnp.log1p(jnp.exp(-jnp.abs(nlam)))
    log_a = (-LRU_C * softplus) * r
    a = jnp.exp(log_a)
    y = 1.0 - a * a
    a_buf[...] = a
    b_buf[...] = jnp.where(y > 0.0, y * lax.rsqrt(y), 0.0) * (i * xc)

    h = hcar[...]
    for t in range(tt):
        grp = slice(t * bsz, (t + 1) * bsz)
        h = a_buf[grp, :] * h + b_buf[grp, :]
        h_buf[grp, :] = h
    hcar[...] = h

    rec_out = h_buf[...] * jax.nn.gelu(yr)

    g_buf[cf_rows:cf_rows + rows, :] = cv * _sigmoid(cg)
    blk_groups = CONV_ROWS // SUBLANES

    def conv_block(rb, carry):
        row0 = pl.multiple_of(rb * CONV_ROWS, CONV_ROWS)
        acc = jnp.broadcast_to(cfb_ref[...], (CONV_ROWS, d_conv)).reshape(blk_groups, SUBLANES, d_conv)
        for k in range(CONV_WIDTH):
            src = g_buf[pl.ds(row0 + k * bsz, CONV_ROWS), :]
            acc = acc + wb_buf[k] * src.reshape(blk_groups, SUBLANES, d_conv)
        c_buf[pl.ds(row0, CONV_ROWS), :] = acc.reshape(CONV_ROWS, d_conv)
        return carry

    lax.fori_loop(0, rows // CONV_ROWS, conv_block, 0, unroll=True)
    cn = _layer_norm(c_buf[...], cfg_ref[...], cfbeta_ref[...])
    c2 = cn * _sigmoid(cn)

    out = _dot(rec_out.astype(BF16), w_out_ref[0:d_rec, :]) + _dot(c2.astype(BF16), w_out_ref[d_rec:d_rec + d_conv, :])
    out_seq = pltpu.einshape("tbd->btd", out.reshape(tt, bsz, d_model))
    y_ref[...] = _layer_norm(alpha * x_ref[...] + out_seq, lng_ref[...], lnb_ref[...])

    new_rc = xr_buf[rows:rows + rc_rows, :]
    new_cf = g_buf[rows:rows + cf_rows, :]
    xr_buf[0:rc_rows, :] = new_rc
    g_buf[0:cf_rows, :] = new_cf
    rc_out_ref[...] = new_rc.reshape(REC_CONV - 1, bsz, d_rec)
    cf_out_ref[...] = new_cf.reshape(CONV_WIDTH - 1, bsz, d_conv)
    h_out_ref[...] = h


def _const_spec(arr):
    zeros = (0,) * arr.ndim
    return pl.BlockSpec(arr.shape, lambda b, t: zeros)


def _mixer_ab_call(x, rc0, h0, cf0, prm, ln_g, ln_b, alpha, tt):
    bsz, seq, d_model = x.shape
    (w_in, rcw, rcb, wgate, gab, gxb, lam, cfw, cfb, cfg, cfbeta, w_out) = prm
    d_rec, d_conv = rcw.shape[1], cfw.shape[1]
    assert bsz == SUBLANES and seq % tt == 0 and (bsz * tt) % CONV_ROWS == 0
    consts = (w_in, rcw, rcb, wgate, gab, gxb, lam, cfw, cfb, cfg, cfbeta, w_out, ln_g, ln_b)
    states = (jnp.swapaxes(rc0, 0, 1), h0, jnp.swapaxes(cf0, 0, 1))

    def whole(arr):
        zeros = (0,) * arr.ndim
        return pl.BlockSpec(arr.shape, lambda s: zeros)

    state_shapes = (
        jax.ShapeDtypeStruct((REC_CONV - 1, bsz, d_rec), F32),
        jax.ShapeDtypeStruct((bsz, d_rec), F32),
        jax.ShapeDtypeStruct((CONV_WIDTH - 1, bsz, d_conv), F32),
    )
    rows = bsz * tt
    scratch = [
        pltpu.VMEM(((REC_CONV - 1) * bsz + rows, d_rec), F32),
        pltpu.VMEM(((CONV_WIDTH - 1) * bsz + rows, d_conv), F32),
        pltpu.VMEM((rows, d_rec), F32),
        pltpu.VMEM((rows, d_rec), F32),
        pltpu.VMEM((rows, d_rec), F32),
        pltpu.VMEM((bsz, d_rec), F32),
        pltpu.VMEM((rows, d_conv), F32),
        pltpu.VMEM((CONV_WIDTH, SUBLANES, d_conv), F32),
    ]
    y, rc, h, cf = pl.pallas_call(
        functools.partial(_mixer_ab_body, alpha),
        grid=(seq // tt,),
        in_specs=[pl.BlockSpec((bsz, tt, d_model), lambda s: (0, s, 0))]
        + [whole(a) for a in states] + [whole(c) for c in consts],
        out_specs=(pl.BlockSpec((bsz, tt, d_model), lambda s: (0, s, 0)),) + tuple(whole(a) for a in state_shapes),
        out_shape=(jax.ShapeDtypeStruct((bsz, seq, d_model), F32),) + state_shapes,
        scratch_shapes=scratch,
        compiler_params=pltpu.CompilerParams(
            dimension_semantics=("arbitrary",), vmem_limit_bytes=VMEM_LIMIT_BYTES),
        name="mixer_ab",
    )(x, *states, *consts)
    return y, jnp.swapaxes(rc, 0, 1), h, jnp.swapaxes(cf, 0, 1)


def _mixer_c_body(alpha, sink_ref, x_ref, k0_ref, v0_ref, wqkv_ref, w_out_ref, lng_ref, lnb_ref,
                  y_ref, k_out_ref, v_out_ref,
                  q_buf, k_buf, v_buf, o_buf, bias_buf, sink_buf):
    bsz, tt, d_model = x_ref.shape
    rows = bsz * tt
    d_q = N_HEADS * HEAD_DIM
    d_kv = N_KV * HEAD_DIM
    win_keys = WINDOW + CHUNK
    stack = GROUP * CHUNK

    row = lax.broadcasted_iota(jnp.int32, (stack, win_keys), 0)
    ko = lax.broadcasted_iota(jnp.int32, (stack, win_keys), 1)
    absdist = jnp.abs(WINDOW + (row % CHUNK) - ko).astype(F32)
    row1 = lax.broadcasted_iota(jnp.int32, (stack, 1), 0)
    for kh in range(N_KV):
        slope = jnp.zeros((stack, win_keys), F32)
        sink = jnp.zeros((stack, 1), F32)
        for g in range(GROUP):
            h = kh * GROUP + g
            slope = jnp.where(row // CHUNK == g, 2.0 ** (-8.0 * (h + 1) / N_HEADS), slope)
            sink = jnp.where(row1 // CHUNK == g, sink_ref[h], sink)
        bias_buf[kh] = -slope * absdist
        sink_buf[kh] = sink

    x = x_ref[...].reshape(rows, d_model)
    xb = x.astype(BF16)
    q_buf[...] = (_dot(xb, wqkv_ref[:, 0:d_q]) * (HEAD_DIM ** -0.5)).astype(BF16)
    k = _dot(xb, wqkv_ref[:, d_q:d_q + d_kv]).reshape(bsz, tt, d_kv)
    v = _dot(xb, wqkv_ref[:, d_q + d_kv:d_q + 2 * d_kv]).reshape(bsz, tt, d_kv)
    k_buf[:, 0:WINDOW, :] = k0_ref[...].astype(BF16)
    v_buf[:, 0:WINDOW, :] = v0_ref[...].astype(BF16)
    k_buf[:, WINDOW:win_keys, :] = k.astype(BF16)
    v_buf[:, WINDOW:win_keys, :] = v.astype(BF16)
    k_out_ref[:, 0:WINDOW - tt, :] = k0_ref[:, tt:WINDOW, :]
    v_out_ref[:, 0:WINDOW - tt, :] = v0_ref[:, tt:WINDOW, :]
    k_out_ref[:, WINDOW - tt:WINDOW, :] = k
    v_out_ref[:, WINDOW - tt:WINDOW, :] = v

    def seq_body(b, carry):
        r0 = pl.multiple_of(b * CHUNK, CHUNK)
        scores = []
        for kh in range(N_KV):
            q4 = q_buf[pl.ds(r0, CHUNK), kh * stack:(kh + 1) * stack]
            qs = jnp.concatenate([q4[:, g * HEAD_DIM:(g + 1) * HEAD_DIM] for g in range(GROUP)], axis=0)
            kwh = k_buf[b, :, kh * HEAD_DIM:(kh + 1) * HEAD_DIM]
            s = lax.dot_general(qs, kwh, (((1,), (1,)), ((), ())), preferred_element_type=F32)
            scores.append(s + bias_buf[kh])
        probs, denoms = [], []
        for kh in range(N_KV):
            s = scores[kh]
            sink = sink_buf[kh]
            m = jnp.maximum(jnp.max(s, axis=-1, keepdims=True), sink)
            p = jnp.exp(s - m)
            denoms.append(jnp.sum(p, axis=-1, keepdims=True) + jnp.exp(sink - m))
            probs.append(p.astype(BF16))
        for kh in range(N_KV):
            vwh = v_buf[b, :, kh * HEAD_DIM:(kh + 1) * HEAD_DIM]
            o = (_dot(probs[kh], vwh) / denoms[kh]).astype(BF16)
            for g in range(GROUP):
                h = kh * GROUP + g
                o_buf[pl.ds(r0, CHUNK), h * HEAD_DIM:(h + 1) * HEAD_DIM] = o[g * CHUNK:(g + 1) * CHUNK, :]
        return carry

    lax.fori_loop(0, bsz, seq_body, 0)

    out = _dot(o_buf[...], w_out_ref[...])
    y_ref[...] = _layer_norm(alpha * x + out, lng_ref[...], lnb_ref[...]).reshape(bsz, tt, d_model)


def _mixer_c_call(x, k0, v0, wqkv, sinks, w_out, ln_g, ln_b, alpha):
    bsz, seq, d_model = x.shape
    d_q = N_HEADS * HEAD_DIM
    d_kv = N_KV * HEAD_DIM
    assert seq == CHUNK and k0.shape == (bsz, WINDOW, d_kv)
    operands = (x, k0, v0, wqkv, w_out, ln_g, ln_b)

    def whole(arr):
        zeros = (0,) * arr.ndim
        return pl.BlockSpec(arr.shape, lambda s: zeros)

    out_shape = (
        jax.ShapeDtypeStruct((bsz, seq, d_model), F32),
        jax.ShapeDtypeStruct((bsz, WINDOW, d_kv), F32),
        jax.ShapeDtypeStruct((bsz, WINDOW, d_kv), F32),
    )
    scratch = [
        pltpu.VMEM((bsz * seq, d_q), BF16),
        pltpu.VMEM((bsz, WINDOW + CHUNK, d_kv), BF16),
        pltpu.VMEM((bsz, WINDOW + CHUNK, d_kv), BF16),
        pltpu.VMEM((bsz * seq, d_q), BF16),
        pltpu.VMEM((N_KV, GROUP * CHUNK, WINDOW + CHUNK), F32),
        pltpu.VMEM((N_KV, GROUP * CHUNK, 1), F32),
    ]
    return pl.pallas_call(
        functools.partial(_mixer_c_body, alpha),
        grid=(1,),
        in_specs=[pl.BlockSpec(memory_space=pltpu.SMEM)] + [whole(a) for a in operands],
        out_specs=tuple(whole(a) for a in out_shape), out_shape=out_shape,
        scratch_shapes=scratch,
        compiler_params=pltpu.CompilerParams(
            dimension_semantics=("arbitrary",), vmem_limit_bytes=VMEM_LIMIT_BYTES),
        name="mixer_c",
    )(sinks, *operands)


QBLK = 2 * CHUNK
KBLK = WINDOW + QBLK
ONES_ROWS = 16
ATTN_SKEW = 4


def _mixer_c_prompt_body(alpha, sink_ref, x_ref, wqT_ref, wk_ref, wv_ref, wvT_ref, w_out_ref, lng_ref, lnb_ref,
                         y_ref, k_out_ref, v_out_ref,
                         qT_buf, k_buf, vT_buf, oT_buf, bias_buf, sink_buf):
    tt = x_ref.shape[0]
    t = pl.program_id(1)
    n_pairs = N_HEADS // 2
    lanes = 2 * QBLK

    @pl.when(t == 0)
    def _():
        vT_buf[...] = jnp.ones(vT_buf.shape, BF16)
        k_buf[:, 0:WINDOW, :] = jnp.zeros((N_KV, WINDOW, HEAD_DIM), BF16)
        kr = lax.broadcasted_iota(jnp.int32, (KBLK, lanes), 0)
        ln = lax.broadcasted_iota(jnp.int32, (KBLK, lanes), 1)
        ql = ln % QBLK
        absdist = jnp.abs(ql + WINDOW - kr).astype(F32)
        kc, qc = kr // CHUNK, ql // CHUNK
        in_window = (kc >= qc) & (kc <= qc + WINDOW // CHUNK)
        ln1 = lax.broadcasted_iota(jnp.int32, (SUBLANES, lanes), 1)
        for pr in range(n_pairs):
            s0 = 2.0 ** (-8.0 * (2 * pr + 1) / N_HEADS)
            s1 = 2.0 ** (-8.0 * (2 * pr + 2) / N_HEADS)
            slope = jnp.where(ln < QBLK, s0, s1)
            bias_buf[pr] = jnp.where(in_window, -slope * absdist, NEG_INF)
            sink_buf[pr] = jnp.where(ln1 < QBLK, sink_ref[2 * pr], sink_ref[2 * pr + 1])

    x = x_ref[...]
    xb = x.astype(BF16)
    nt_dims = (((1,), (1,)), ((), ()))
    qT = lax.dot_general(wqT_ref[...], xb, nt_dims, preferred_element_type=F32)
    qT_buf[...] = (qT * (HEAD_DIM ** -0.5)).astype(BF16)
    k = _dot(xb, wk_ref[...])
    kb = k.astype(BF16)
    vT = lax.dot_general(wvT_ref[...], xb, nt_dims, preferred_element_type=F32)
    for kh in range(N_KV):
        k_buf[kh, WINDOW:WINDOW + tt, :] = kb[:, kh * HEAD_DIM:(kh + 1) * HEAD_DIM]
        vT_buf[kh, 0:HEAD_DIM, WINDOW:WINDOW + tt] = vT[kh * HEAD_DIM:(kh + 1) * HEAD_DIM, :].astype(BF16)
    k_out_ref[...] = k[tt - WINDOW:tt, :]

    @pl.when(t == pl.num_programs(1) - 1)
    def _():
        v_out_ref[...] = _dot(xb[tt - WINDOW:tt, :], wv_ref[...])

    def scores(qb, pr, key_lo):
        kh = pr // (GROUP // 2)
        q_lanes = slice(qb * QBLK, (qb + 1) * QBLK)
        h_rows = [slice((2 * pr + i) * HEAD_DIM, (2 * pr + i + 1) * HEAD_DIM) for i in range(2)]
        q2 = jnp.concatenate([qT_buf[h_rows[0], q_lanes], qT_buf[h_rows[1], q_lanes]], axis=1)
        keys = slice(qb * QBLK + key_lo, qb * QBLK + KBLK)
        return _dot(k_buf[kh, keys, :], q2) + bias_buf[pr, key_lo:KBLK, :]

    def finish(qb, pr, key_lo, sT):
        kh = pr // (GROUP // 2)
        q_lanes = slice(qb * QBLK, (qb + 1) * QBLK)
        h_rows = [slice((2 * pr + i) * HEAD_DIM, (2 * pr + i + 1) * HEAD_DIM) for i in range(2)]
        keys = slice(qb * QBLK + key_lo, qb * QBLK + KBLK)
        sink = sink_buf[pr][0:1, :]
        m = jnp.maximum(jnp.max(sT, axis=0, keepdims=True), sink)
        p = jnp.exp(sT - m)
        pv = _dot(vT_buf[kh, :, keys], p.astype(BF16))
        den = pv[HEAD_DIM:HEAD_DIM + 1, :] + jnp.exp(sink - m)
        oT = (pv[0:HEAD_DIM, :] * (1.0 / den)).astype(BF16)
        oT_buf[h_rows[0], q_lanes] = oT[:, 0:QBLK]
        oT_buf[h_rows[1], q_lanes] = oT[:, QBLK:lanes]

    def attend_all(first_key_lo):
        items = [(qb, pr, first_key_lo if qb == 0 else 0) for qb in range(tt // QBLK) for pr in range(n_pairs)]
        pending = []
        for item in items:
            pending.append((item, scores(*item)))
            if len(pending) > ATTN_SKEW:
                done, sT = pending.pop(0)
                finish(*done, sT)
        for done, sT in pending:
            finish(*done, sT)

    @pl.when(t == 0)
    def _():
        attend_all(WINDOW)

    @pl.when(t != 0)
    def _():
        attend_all(0)

    out = lax.dot_general(oT_buf[...], w_out_ref[...], (((0,), (0,)), ((), ())), preferred_element_type=F32)
    y_ref[...] = _layer_norm(alpha * x + out, lng_ref[...], lnb_ref[...])
    k_buf[:, 0:WINDOW, :] = k_buf[:, tt:tt + WINDOW, :]
    vT_buf[:, 0:HEAD_DIM, 0:WINDOW] = vT_buf[:, 0:HEAD_DIM, tt:tt + WINDOW]


def _mixer_c_prompt_call(x, wqT, wk, wv, wvT, sinks, w_out, ln_g, ln_b, alpha, tt):
    bsz, seq, d_model = x.shape
    d_q = N_HEADS * HEAD_DIM
    d_kv = N_KV * HEAD_DIM
    assert seq % tt == 0 and tt % QBLK == 0 and tt >= WINDOW
    consts = (wqT, wk, wv, wvT, w_out, ln_g, ln_b)
    in_specs = [
        pl.BlockSpec(memory_space=pltpu.SMEM),
        pl.BlockSpec((None, tt, d_model), lambda b, t: (b, t, 0)),
    ] + [_const_spec(c) for c in consts]
    out_shape = (
        jax.ShapeDtypeStruct((bsz, seq, d_model), F32),
        jax.ShapeDtypeStruct((bsz, WINDOW, d_kv), F32),
        jax.ShapeDtypeStruct((bsz, WINDOW, d_kv), F32),
    )
    out_specs = (
        pl.BlockSpec((None, tt, d_model), lambda b, t: (b, t, 0)),
        pl.BlockSpec((None, WINDOW, d_kv), lambda b, t: (b, 0, 0)),
        pl.BlockSpec((None, WINDOW, d_kv), lambda b, t: (b, 0, 0)),
    )
    scratch = [
        pltpu.VMEM((d_q, tt), BF16),
        pltpu.VMEM((N_KV, tt + WINDOW, HEAD_DIM), BF16),
        pltpu.VMEM((N_KV, HEAD_DIM + ONES_ROWS, tt + WINDOW), BF16),
        pltpu.VMEM((d_q, tt), BF16),
        pltpu.VMEM((N_HEADS // 2, KBLK, 2 * QBLK), F32),
        pltpu.VMEM((N_HEADS // 2, SUBLANES, 2 * QBLK), F32),
    ]
    return pl.pallas_call(
        functools.partial(_mixer_c_prompt_body, alpha),
        grid=(bsz, seq // tt), in_specs=in_specs, out_specs=out_specs, out_shape=out_shape,
        scratch_shapes=scratch,
        compiler_params=pltpu.CompilerParams(
            dimension_semantics=("arbitrary", "arbitrary"), vmem_limit_bytes=VMEM_LIMIT_BYTES),
        name="mixer_c_prompt",
    )(sinks, x, *consts)


FFN_CHUNK = MXU_DIM


def _ffn_body(alpha, x_ref, wg_ref, wu_ref, wd_ref, lng_ref, lnb_ref, y_ref, xb_buf, acc_buf):
    n_chunks = wg_ref.shape[1] // FFN_CHUNK
    xb_buf[...] = x_ref[...].astype(BF16)

    def down(c):
        cols = slice(c * FFN_CHUNK, (c + 1) * FFN_CHUNK)
        xb = xb_buf[...]
        g = _dot(xb, wg_ref[:, cols])
        u = _dot(xb, wu_ref[:, cols])
        hid = (g * _sigmoid(g) * u).astype(BF16)
        return _dot(hid, wd_ref[cols, :])

    acc_buf[...] = alpha * x_ref[...] + down(0)
    for c in range(1, n_chunks):
        acc_buf[...] += down(c)
    y_ref[...] = _layer_norm(acc_buf[...], lng_ref[...], lnb_ref[...])


def _ffn_call(x2d, layer, wg, wu, wd, ln_g, ln_b, alpha, tm):
    rows, d_model = x2d.shape
    assert rows % tm == 0 and wg.shape[2] % FFN_CHUNK == 0
    consts = (wg, wu, wd, ln_g, ln_b)

    def const_spec(arr):
        zeros = (0,) * (arr.ndim - 1)
        return pl.BlockSpec((None,) + arr.shape[1:], lambda i: (layer,) + zeros, pipeline_mode=pl.Buffered(1))

    return pl.pallas_call(
        functools.partial(_ffn_body, alpha),
        grid=(rows // tm,),
        in_specs=[pl.BlockSpec((tm, d_model), lambda i: (i, 0))] + [const_spec(c) for c in consts],
        out_specs=pl.BlockSpec((tm, d_model), lambda i: (i, 0)),
        out_shape=jax.ShapeDtypeStruct((rows, d_model), F32),
        scratch_shapes=[pltpu.VMEM((tm, d_model), BF16), pltpu.VMEM((tm, d_model), F32)],
        compiler_params=pltpu.CompilerParams(
            dimension_semantics=("arbitrary",), vmem_limit_bytes=VMEM_LIMIT_BYTES),
        name="ffn",
    )(x2d, *consts)


def _ffn(x, layer, ffn_prm, alpha, tm):
    d_model = x.shape[-1]
    return _ffn_call(x.reshape(-1, d_model), layer, *ffn_prm, alpha, tm).reshape(x.shape)


def _row(v):
    return v.reshape(1, -1).astype(F32)


def _gate_tiles(ga_w, gx_w):
    heads, blk, _ = ga_w.shape
    per_tile = MXU_DIM // blk
    tiles = []
    for j in range(heads // per_tile):
        def diag(w):
            return jax.scipy.linalg.block_diag(*[w[j * per_tile + i] for i in range(per_tile)])
        tiles.append(jnp.concatenate([diag(ga_w), diag(gx_w)], axis=1))
    return jnp.stack(tiles).astype(BF16)


def kernel(x_prompt, x_sample, state_rec_h, state_rec_conv, state_cf_conv, cache_k, cache_v, w_in_ab, rec_conv_w, rec_conv_b, rec_gate_a_w, rec_gate_a_b, rec_gate_x_w, rec_gate_x_b, rec_lambda, cf_conv_w, cf_conv_b, cf_norm_g, cf_norm_b, w_out_ab, w_qkv, attn_sinks, w_out_c, ln_mix_g, ln_mix_b, w_ff_gate, w_ff_up, w_ff_down, ln_ff_g, ln_ff_b):
    depth = ln_mix_g.shape[0]
    alpha = (2 * depth) ** 0.25
    bp, seq_p, d_model = x_prompt.shape
    bs, seq_s, _ = x_sample.shape
    d_rec = rec_conv_w.shape[-1]
    d_conv = cf_conv_w.shape[-1]
    d_kv = N_KV * HEAD_DIM
    tt_ab = min(seq_p, 64)
    tt_c = min(seq_p, 1024)
    tt_s = seq_s
    tm_p = min(bp * seq_p, 1024)
    tm_s = bs * seq_s

    ffn_prm = (w_ff_gate.astype(BF16), w_ff_up.astype(BF16), w_ff_down.astype(BF16),
               ln_ff_g.reshape(depth, 1, d_model).astype(F32), ln_ff_b.reshape(depth, 1, d_model).astype(F32))

    yp, ys = x_prompt, x_sample
    p_h, s_h, p_rc, s_rc, p_cf, s_cf, p_k, s_k, p_v, s_v = ([] for _ in range(10))
    for layer in range(depth):
        j = layer // 2
        ln_g, ln_b = _row(ln_mix_g[layer]), _row(ln_mix_b[layer])
        if layer % 2 == 0:
            prm =(w_in_ab[j].astype(BF16), rec_conv_w[j], _row(rec_conv_b[j]),
                   _gate_tiles(rec_gate_a_w[j], rec_gate_x_w[j]), _row(rec_gate_a_b[j]), _row(rec_gate_x_b[j]),
                   _row(rec_lambda[j]), cf_conv_w[j], _row(cf_conv_b[j]), _row(cf_norm_g[j]), _row(cf_norm_b[j]),
                   w_out_ab[j].astype(BF16))
            yp, rc, hl, cf = _mixer_ab_call(
                yp, jnp.zeros((bp, REC_CONV - 1, d_rec), F32), jnp.zeros((bp, d_rec), F32),
                jnp.zeros((bp, CONV_WIDTH - 1, d_conv), F32), prm, ln_g, ln_b, alpha, tt_ab)
            p_rc.append(rc), p_h.append(hl), p_cf.append(cf)
            ys, rc, hl, cf = _mixer_ab_call(
                ys, state_rec_conv[j], state_rec_h[j], state_cf_conv[j], prm, ln_g, ln_b, alpha, tt_s)
            s_rc.append(rc), s_h.append(hl), s_cf.append(cf)
        else:
            wqkv, w_out = w_qkv[j].astype(BF16), w_out_c[j].astype(BF16)
            sinks = attn_sinks[j].astype(F32)
            d_q = N_HEADS * HEAD_DIM
            wq_t = wqkv[:, 0:d_q].T
            wk, wv = wqkv[:, d_q:d_q + d_kv], wqkv[:, d_q + d_kv:d_q + 2 * d_kv]
            yp, kk, vv = _mixer_c_prompt_call(yp, wq_t, wk, wv, wv.T, sinks, w_out, ln_g, ln_b, alpha, tt_c)
            p_k.append(kk.reshape(bp, WINDOW, N_KV, HEAD_DIM)), p_v.append(vv.reshape(bp, WINDOW, N_KV, HEAD_DIM))
            ck = cache_k[j].reshape(bs, WINDOW, d_kv)
            cv = cache_v[j].reshape(bs, WINDOW, d_kv)
            ys, kk, vv = _mixer_c_call(ys, ck, cv, wqkv, sinks, w_out, ln_g, ln_b, alpha)
            s_k.append(kk.reshape(bs, WINDOW, N_KV, HEAD_DIM)), s_v.append(vv.reshape(bs, WINDOW, N_KV, HEAD_DIM))
        yp = _ffn(yp, layer, ffn_prm, alpha, tm_p)
        ys = _ffn(ys, layer, ffn_prm, alpha, tm_s)
    return (yp, ys, jnp.stack(p_h), jnp.stack(s_h), jnp.stack(p_rc), jnp.stack(s_rc),
            jnp.stack(p_cf), jnp.stack(s_cf), jnp.stack(p_k), jnp.stack(s_k), jnp.stack(p_v), jnp.stack(s_v))
```

```python
import functools

import jax
import jax.numpy as jnp
from jax import lax
from jax.experimental import pallas as pl
from jax.experimental.pallas import tpu as pltpu

CHUNK = 64
WINDOW = 128
HEAD_DIM = 64
N_HEADS = 16
N_KV = 4
GROUP = N_HEADS // N_KV
REC_CONV = 4
CONV_WIDTH = 31
LRU_C = 8.0
LN_EPS = 1e-5
NEG_INF = -1e30

SUBLANES = 8
MXU_DIM = 256
CONV_ROWS = 16
VMEM_LIMIT_BYTES = 56 * 1024 * 1024

BF16 = jnp.bfloat16
F32 = jnp.float32


def _dot(a, b):
    return jnp.dot(a, b, preferred_element_type=F32)


def _layer_norm(z, g, b):
    mu = jnp.mean(z, axis=-1, keepdims=True)
    zc = z - mu
    var = jnp.mean(zc * zc, axis=-1, keepdims=True)
    return zc * lax.rsqrt(var + LN_EPS) * g + b


def _sigmoid(x):
    return jax.nn.sigmoid(x)


def _mixer_ab_body(alpha, x_ref, rc0_ref, h0_ref, cf0_ref, w_in_ref, rcw_ref, rcb_ref, wgate_ref,
                   gab_ref, gxb_ref, lam_ref, cfw_ref, cfb_ref, cfg_ref, cfbeta_ref, w_out_ref,
                   lng_ref, lnb_ref,
                   y_ref, rc_out_ref, h_out_ref, cf_out_ref,
                   xr_buf, g_buf, a_buf, b_buf, h_buf, hcar, c_buf, wb_buf):
    bsz, tt, d_model = x_ref.shape
    rows = bsz * tt
    d_rec = rcw_ref.shape[1]
    d_conv = cfw_ref.shape[1]
    rc_rows = (REC_CONV - 1) * bsz
    cf_rows = (CONV_WIDTH - 1) * bsz
    step = pl.program_id(0)

    @pl.when(step == 0)
    def _():
        xr_buf[0:rc_rows, :] = rc0_ref[...].reshape(rc_rows, d_rec)
        g_buf[0:cf_rows, :] = cf0_ref[...].reshape(cf_rows, d_conv)
        hcar[...] = h0_ref[...]
        for k in range(CONV_WIDTH):
            wb_buf[k] = jnp.broadcast_to(cfw_ref[k:k + 1, :], (SUBLANES, d_conv))

    xb = pltpu.einshape("btd->tbd", x_ref[...].astype(BF16)).reshape(rows, d_model)

    xr = _dot(xb, w_in_ref[:, 0:d_rec])
    yr = _dot(xb, w_in_ref[:, d_rec:2 * d_rec])
    cv = _dot(xb, w_in_ref[:, 2 * d_rec:2 * d_rec + d_conv])
    cg = _dot(xb, w_in_ref[:, 2 * d_rec + d_conv:2 * d_rec + 2 * d_conv])

    xr_buf[rc_rows:rc_rows + rows, :] = xr
    xc = rcb_ref[...] + rcw_ref[REC_CONV - 1:REC_CONV, :] * xr
    for k in range(REC_CONV - 1):
        xc = xc + rcw_ref[k:k + 1, :] * xr_buf[k * bsz:k * bsz + rows, :]
    xcb = xc.astype(BF16)
    r_parts, i_parts = [], []
    for j in range(d_rec // MXU_DIM):
        lo, hi = j * MXU_DIM, (j + 1) * MXU_DIM
        gj = _dot(xcb[:, lo:hi], wgate_ref[j])
        r_parts.append(_sigmoid(gj[:, :MXU_DIM] + gab_ref[:, lo:hi]))
        i_parts.append(_sigmoid(gj[:, MXU_DIM:] + gxb_ref[:, lo:hi]))
    r = jnp.concatenate(r_parts, axis=1)
    i = jnp.concatenate(i_parts, axis=1)
    nlam = -lam_ref[...]
    softplus = jnp.maximum(nlam, 0.0) + jnp.log1p(jnp.exp(-jnp.abs(nlam)))
    log_a = (-LRU_C * softplus) * r
    a = jnp.exp(log_a)
    y = 1.0 - a * a
    a_buf[...] = a
    b_buf[...] = jnp.where(y > 0.0, y * lax.rsqrt(y), 0.0) * (i * xc)

    h = hcar[...]
    for t in range(tt):
        grp = slice(t * bsz, (t + 1) * bsz)
        h = a_buf[grp, :] * h + b_buf[grp, :]
        h_buf[grp, :] = h
    hcar[...] = h

    rec_out = h_buf[...] * jax.nn.gelu(yr)

    g_buf[cf_rows:cf_rows + rows, :] = cv * _sigmoid(cg)
    blk_groups = CONV_ROWS // SUBLANES

    def conv_block(rb, carry):
        row0 = pl.multiple_of(rb * CONV_ROWS, CONV_ROWS)
        acc = jnp.broadcast_to(cfb_ref[...], (CONV_ROWS, d_conv)).reshape(blk_groups, SUBLANES, d_conv)
        for k in range(CONV_WIDTH):
            src = g_buf[pl.ds(row0 + k * bsz, CONV_ROWS), :]
            acc = acc + wb_buf[k] * src.reshape(blk_groups, SUBLANES, d_conv)
        c_buf[pl.ds(row0, CONV_ROWS), :] = acc.reshape(CONV_ROWS, d_conv)
        return carry

    lax.fori_loop(0, rows // CONV_ROWS, conv_block, 0, unroll=True)
    cn = _layer_norm(c_buf[...], cfg_ref[...], cfbeta_ref[...])
    c2 = cn * _sigmoid(cn)

    out = _dot(rec_out.astype(BF16), w_out_ref[0:d_rec, :]) + _dot(c2.astype(BF16), w_out_ref[d_rec:d_rec + d_conv, :])
    out_seq = pltpu.einshape("tbd->btd", out.reshape(tt, bsz, d_model))
    y_ref[...] = _layer_norm(alpha * x_ref[...] + out_seq, lng_ref[...], lnb_ref[...])

    new_rc = xr_buf[rows:rows + rc_rows, :]
    new_cf = g_buf[rows:rows + cf_rows, :]
    xr_buf[0:rc_rows, :] = new_rc
    g_buf[0:cf_rows, :] = new_cf
    rc_out_ref[...] = new_rc.reshape(REC_CONV - 1, bsz, d_rec)
    cf_out_ref[...] = new_cf.reshape(CONV_WIDTH - 1, bsz, d_conv)
    h_out_ref[...] = h


def _const_spec(arr):
    zeros = (0,) * arr.ndim
    return pl.BlockSpec(arr.shape, lambda b, t: zeros)


def _mixer_ab_call(x, rc0, h0, cf0, prm, ln_g, ln_b, alpha, tt):
    bsz, seq, d_model = x.shape
    (w_in, rcw, rcb, wgate, gab, gxb, lam, cfw, cfb, cfg, cfbeta, w_out) = prm
    d_rec, d_conv = rcw.shape[1], cfw.shape[1]
    assert bsz == SUBLANES and seq % tt == 0 and (bsz * tt) % CONV_ROWS == 0
    consts = (w_in, rcw, rcb, wgate, gab, gxb, lam, cfw, cfb, cfg, cfbeta, w_out, ln_g, ln_b)
    states = (jnp.swapaxes(rc0, 0, 1), h0, jnp.swapaxes(cf0, 0, 1))

    def whole(arr):
        zeros = (0,) * arr.ndim
        return pl.BlockSpec(arr.shape, lambda s: zeros)

    state_shapes = (
        jax.ShapeDtypeStruct((REC_CONV - 1, bsz, d_rec), F32),
        jax.ShapeDtypeStruct((bsz, d_rec), F32),
        jax.ShapeDtypeStruct((CONV_WIDTH - 1, bsz, d_conv), F32),
    )
    rows = bsz * tt
    scratch = [
        pltpu.VMEM(((REC_CONV - 1) * bsz + rows, d_rec), F32),
        pltpu.VMEM(((CONV_WIDTH - 1) * bsz + rows, d_conv), F32),
        pltpu.VMEM((rows, d_rec), F32),
        pltpu.VMEM((rows, d_rec), F32),
        pltpu.VMEM((rows, d_rec), F32),
        pltpu.VMEM((bsz, d_rec), F32),
        pltpu.VMEM((rows, d_conv), F32),
        pltpu.VMEM((CONV_WIDTH, SUBLANES, d_conv), F32),
    ]
    y, rc, h, cf = pl.pallas_call(
        functools.partial(_mixer_ab_body, alpha),
        grid=(seq // tt,),
        in_specs=[pl.BlockSpec((bsz, tt, d_model), lambda s: (0, s, 0))]
        + [whole(a) for a in states] + [whole(c) for c in consts],
        out_specs=(pl.BlockSpec((bsz, tt, d_model), lambda s: (0, s, 0)),) + tuple(whole(a) for a in state_shapes),
        out_shape=(jax.ShapeDtypeStruct((bsz, seq, d_model), F32),) + state_shapes,
        scratch_shapes=scratch,
        compiler_params=pltpu.CompilerParams(
            dimension_semantics=("arbitrary",), vmem_limit_bytes=VMEM_LIMIT_BYTES),
        name="mixer_ab",
    )(x, *states, *consts)
    return y, jnp.swapaxes(rc, 0, 1), h, jnp.swapaxes(cf, 0, 1)


def _mixer_c_body(alpha, sink_ref, x_ref, k0_ref, v0_ref, wqkv_ref, w_out_ref, lng_ref, lnb_ref,
                  y_ref, k_out_ref, v_out_ref,
                  q_buf, k_buf, v_buf, o_buf, bias_buf, sink_buf):
    bsz, tt, d_model = x_ref.shape
    rows = bsz * tt
    d_q = N_HEADS * HEAD_DIM
    d_kv = N_KV * HEAD_DIM
    win_keys = WINDOW + CHUNK
    stack = GROUP * CHUNK

    row = lax.broadcasted_iota(jnp.int32, (stack, win_keys), 0)
    ko = lax.broadcasted_iota(jnp.int32, (stack, win_keys), 1)
    absdist = jnp.abs(WINDOW + (row % CHUNK) - ko).astype(F32)
    row1 = lax.broadcasted_iota(jnp.int32, (stack, 1), 0)
    for kh in range(N_KV):
        slope = jnp.zeros((stack, win_keys), F32)
        sink = jnp.zeros((stack, 1), F32)
        for g in range(GROUP):
            h = kh * GROUP + g
            slope = jnp.where(row // CHUNK == g, 2.0 ** (-8.0 * (h + 1) / N_HEADS), slope)
            sink = jnp.where(row1 // CHUNK == g, sink_ref[h], sink)
        bias_buf[kh] = -slope * absdist
        sink_buf[kh] = sink

    x = x_ref[...].reshape(rows, d_model)
    xb = x.astype(BF16)
    q_buf[...] = (_dot(xb, wqkv_ref[:, 0:d_q]) * (HEAD_DIM ** -0.5)).astype(BF16)
    k = _dot(xb, wqkv_ref[:, d_q:d_q + d_kv]).reshape(bsz, tt, d_kv)
    v = _dot(xb, wqkv_ref[:, d_q + d_kv:d_q + 2 * d_kv]).reshape(bsz, tt, d_kv)
    k_buf[:, 0:WINDOW, :] = k0_ref[...].astype(BF16)
    v_buf[:, 0:WINDOW, :] = v0_ref[...].astype(BF16)
    k_buf[:, WINDOW:win_keys, :] = k.astype(BF16)
    v_buf[:, WINDOW:win_keys, :] = v.astype(BF16)
    k_out_ref[:, 0:WINDOW - tt, :] = k0_ref[:, tt:WINDOW, :]
    v_out_ref[:, 0:WINDOW - tt, :] = v0_ref[:, tt:WINDOW, :]
    k_out_ref[:, WINDOW - tt:WINDOW, :] = k
    v_out_ref[:, WINDOW - tt:WINDOW, :] = v

    def seq_body(b, carry):
        r0 = pl.multiple_of(b * CHUNK, CHUNK)
        scores = []
        for kh in range(N_KV):
            q4 = q_buf[pl.ds(r0, CHUNK), kh * stack:(kh + 1) * stack]
            qs = jnp.concatenate([q4[:, g * HEAD_DIM:(g + 1) * HEAD_DIM] for g in range(GROUP)], axis=0)
            kwh = k_buf[b, :, kh * HEAD_DIM:(kh + 1) * HEAD_DIM]
            s = lax.dot_general(qs, kwh, (((1,), (1,)), ((), ())), preferred_element_type=F32)
            scores.append(s + bias_buf[kh])
        probs, denoms = [], []
        for kh in range(N_KV):
            s = scores[kh]
            sink = sink_buf[kh]
            m = jnp.maximum(jnp.max(s, axis=-1, keepdims=True), sink)
            p = jnp.exp(s - m)
            denoms.append(jnp.sum(p, axis=-1, keepdims=True) + jnp.exp(sink - m))
            probs.append(p.astype(BF16))
        for kh in range(N_KV):
            vwh = v_buf[b, :, kh * HEAD_DIM:(kh + 1) * HEAD_DIM]
            o = (_dot(probs[kh], vwh) / denoms[kh]).astype(BF16)
            for g in range(GROUP):
                h = kh * GROUP + g
                o_buf[pl.ds(r0, CHUNK), h * HEAD_DIM:(h + 1) * HEAD_DIM] = o[g * CHUNK:(g + 1) * CHUNK, :]
        return carry

    lax.fori_loop(0, bsz, seq_body, 0)

    out = _dot(o_buf[...], w_out_ref[...])
    y_ref[...] = _layer_norm(alpha * x + out, lng_ref[...], lnb_ref[...]).reshape(bsz, tt, d_model)


def _mixer_c_call(x, k0, v0, wqkv, sinks, w_out, ln_g, ln_b, alpha):
    bsz, seq, d_model = x.shape
    d_q = N_HEADS * HEAD_DIM
    d_kv = N_KV * HEAD_DIM
    assert seq == CHUNK and k0.shape == (bsz, WINDOW, d_kv)
    operands = (x, k0, v0, wqkv, w_out, ln_g, ln_b)

    def whole(arr):
        zeros = (0,) * arr.ndim
        return pl.BlockSpec(arr.shape, lambda s: zeros)

    out_shape = (
        jax.ShapeDtypeStruct((bsz, seq, d_model), F32),
        jax.ShapeDtypeStruct((bsz, WINDOW, d_kv), F32),
        jax.ShapeDtypeStruct((bsz, WINDOW, d_kv), F32),
    )
    scratch = [
        pltpu.VMEM((bsz * seq, d_q), BF16),
        pltpu.VMEM((bsz, WINDOW + CHUNK, d_kv), BF16),
        pltpu.VMEM((bsz, WINDOW + CHUNK, d_kv), BF16),
        pltpu.VMEM((bsz * seq, d_q), BF16),
        pltpu.VMEM((N_KV, GROUP * CHUNK, WINDOW + CHUNK), F32),
        pltpu.VMEM((N_KV, GROUP * CHUNK, 1), F32),
    ]
    return pl.pallas_call(
        functools.partial(_mixer_c_body, alpha),
        grid=(1,),
        in_specs=[pl.BlockSpec(memory_space=pltpu.SMEM)] + [whole(a) for a in operands],
        out_specs=tuple(whole(a) for a in out_shape), out_shape=out_shape,
        scratch_shapes=scratch,
        compiler_params=pltpu.CompilerParams(
            dimension_semantics=("arbitrary",), vmem_limit_bytes=VMEM_LIMIT_BYTES),
        name="mixer_c",
    )(sinks, *operands)


QBLK = 2 * CHUNK
KBLK = WINDOW + QBLK
ONES_ROWS = 16
ATTN_SKEW = 4


def _mixer_c_prompt_body(alpha, sink_ref, x_ref, wqT_ref, wk_ref, wv_ref, wvT_ref, w_out_ref, lng_ref, lnb_ref,
                         y_ref, k_out_ref, v_out_ref,
                         qT_buf, k_buf, vT_buf, oT_buf, bias_buf, sink_buf):
    tt = x_ref.shape[0]
    t = pl.program_id(1)
    n_pairs = N_HEADS // 2
    lanes = 2 * QBLK

    @pl.when(t == 0)
    def _():
        vT_buf[...] = jnp.ones(vT_buf.shape, BF16)
        k_buf[:, 0:WINDOW, :] = jnp.zeros((N_KV, WINDOW, HEAD_DIM), BF16)
        kr = lax.broadcasted_iota(jnp.int32, (KBLK, lanes), 0)
        ln = lax.broadcasted_iota(jnp.int32, (KBLK, lanes), 1)
        ql = ln % QBLK
        absdist = jnp.abs(ql + WINDOW - kr).astype(F32)
        kc, qc = kr // CHUNK, ql // CHUNK
        in_window = (kc >= qc) & (kc <= qc + WINDOW // CHUNK)
        ln1 = lax.broadcasted_iota(jnp.int32, (SUBLANES, lanes), 1)
        for pr in range(n_pairs):
            s0 = 2.0 ** (-8.0 * (2 * pr + 1) / N_HEADS)
            s1 = 2.0 ** (-8.0 * (2 * pr + 2) / N_HEADS)
            slope = jnp.where(ln < QBLK, s0, s1)
            bias_buf[pr] = jnp.where(in_window, -slope * absdist, NEG_INF)
            sink_buf[pr] = jnp.where(ln1 < QBLK, sink_ref[2 * pr], sink_ref[2 * pr + 1])

    x = x_ref[...]
    xb = x.astype(BF16)
    nt_dims = (((1,), (1,)), ((), ()))
    qT = lax.dot_general(wqT_ref[...], xb, nt_dims, preferred_element_type=F32)
    qT_buf[...] = (qT * (HEAD_DIM ** -0.5)).astype(BF16)
    k = _dot(xb, wk_ref[...])
    kb = k.astype(BF16)
    vT = lax.dot_general(wvT_ref[...], xb, nt_dims, preferred_element_type=F32)
    for kh in range(N_KV):
        k_buf[kh, WINDOW:WINDOW + tt, :] = kb[:, kh * HEAD_DIM:(kh + 1) * HEAD_DIM]
        vT_buf[kh, 0:HEAD_DIM, WINDOW:WINDOW + tt] = vT[kh * HEAD_DIM:(kh + 1) * HEAD_DIM, :].astype(BF16)
    k_out_ref[...] = k[tt - WINDOW:tt, :]

    @pl.when(t == pl.num_programs(1) - 1)
    def _():
        v_out_ref[...] = _dot(xb[tt - WINDOW:tt, :], wv_ref[...])

    def scores(qb, pr, key_lo):
        kh = pr // (GROUP // 2)
        q_lanes = slice(qb * QBLK, (qb + 1) * QBLK)
        h_rows = [slice((2 * pr + i) * HEAD_DIM, (2 * pr + i + 1) * HEAD_DIM) for i in range(2)]
        q2 = jnp.concatenate([qT_buf[h_rows[0], q_lanes], qT_buf[h_rows[1], q_lanes]], axis=1)
        keys = slice(qb * QBLK + key_lo, qb * QBLK + KBLK)
        return _dot(k_buf[kh, keys, :], q2) + bias_buf[pr, key_lo:KBLK, :]

    def finish(qb, pr, key_lo, sT):
        kh = pr // (GROUP // 2)
        q_lanes = slice(qb * QBLK, (qb + 1) * QBLK)
        h_rows = [slice((2 * pr + i) * HEAD_DIM, (2 * pr + i + 1) * HEAD_DIM) for i in range(2)]
        keys = slice(qb * QBLK + key_lo, qb * QBLK + KBLK)
        sink = sink_buf[pr][0:1, :]
        m = jnp.maximum(jnp.max(sT, axis=0, keepdims=True), sink)
        p = jnp.exp(sT - m)
        pv = _dot(vT_buf[kh, :, keys], p.astype(BF16))
        den = pv[HEAD_DIM:HEAD_DIM + 1, :] + jnp.exp(sink - m)
        oT = (pv[0:HEAD_DIM, :] * (1.0 / den)).astype(BF16)
        oT_buf[h_rows[0], q_lanes] = oT[:, 0:QBLK]
        oT_buf[h_rows[1], q_lanes] = oT[:, QBLK:lanes]

    def attend_all(first_key_lo):
        items = [(qb, pr, first_key_lo if qb == 0 else 0) for qb in range(tt // QBLK) for pr in range(n_pairs)]
        pending = []
        for item in items:
            pending.append((item, scores(*item)))
            if len(pending) > ATTN_SKEW:
                done, sT = pending.pop(0)
                finish(*done, sT)
        for done, sT in pending:
            finish(*done, sT)

    @pl.when(t == 0)
    def _():
        attend_all(WINDOW)

    @pl.when(t != 0)
    def _():
        attend_all(0)

    out = lax.dot_general(oT_buf[...], w_out_ref[...], (((0,), (0,)), ((), ())), preferred_element_type=F32)
    y_ref[...] = _layer_norm(alpha * x + out, lng_ref[...], lnb_ref[...])
    k_buf[:, 0:WINDOW, :] = k_buf[:, tt:tt + WINDOW, :]
    vT_buf[:, 0:HEAD_DIM, 0:WINDOW] = vT_buf[:, 0:HEAD_DIM, tt:tt + WINDOW]


def _mixer_c_prompt_call(x, wqT, wk, wv, wvT, sinks, w_out, ln_g, ln_b, alpha, tt):
    bsz, seq, d_model = x.shape
    d_q = N_HEADS * HEAD_DIM
    d_kv = N_KV * HEAD_DIM
    assert seq % tt == 0 and tt % QBLK == 0 and tt >= WINDOW
    consts = (wqT, wk, wv, wvT, w_out, ln_g, ln_b)
    in_specs = [
        pl.BlockSpec(memory_space=pltpu.SMEM),
        pl.BlockSpec((None, tt, d_model), lambda b, t: (b, t, 0)),
    ] + [_const_spec(c) for c in consts]
    out_shape = (
        jax.ShapeDtypeStruct((bsz, seq, d_model), F32),
        jax.ShapeDtypeStruct((bsz, WINDOW, d_kv), F32),
        jax.ShapeDtypeStruct((bsz, WINDOW, d_kv), F32),
    )
    out_specs = (
        pl.BlockSpec((None, tt, d_model), lambda b, t: (b, t, 0)),
        pl.BlockSpec((None, WINDOW, d_kv), lambda b, t: (b, 0, 0)),
        pl.BlockSpec((None, WINDOW, d_kv), lambda b, t: (b, 0, 0)),
    )
    scratch = [
        pltpu.VMEM((d_q, tt), BF16),
        pltpu.VMEM((N_KV, tt + WINDOW, HEAD_DIM), BF16),
        pltpu.VMEM((N_KV, HEAD_DIM + ONES_ROWS, tt + WINDOW), BF16),
        pltpu.VMEM((d_q, tt), BF16),
        pltpu.VMEM((N_HEADS // 2, KBLK, 2 * QBLK), F32),
        pltpu.VMEM((N_HEADS // 2, SUBLANES, 2 * QBLK), F32),
    ]
    return pl.pallas_call(
        functools.partial(_mixer_c_prompt_body, alpha),
        grid=(bsz, seq // tt), in_specs=in_specs, out_specs=out_specs, out_shape=out_shape,
        scratch_shapes=scratch,
        compiler_params=pltpu.CompilerParams(
            dimension_semantics=("arbitrary", "arbitrary"), vmem_limit_bytes=VMEM_LIMIT_BYTES),
        name="mixer_c_prompt",
    )(sinks, x, *consts)


FFN_CHUNK = MXU_DIM
FFN_NORM_SLICES = 8


def _ffn_body(alpha, x_ref, zero_ref, wg_ref, wu_ref, wd_ref, lng_ref, lnb_ref, y_ref, xb_buf, acc_buf):
    n_chunks = wg_ref.shape[1] // FFN_CHUNK
    tm = x_ref.shape[0]
    rs = tm // FFN_NORM_SLICES
    i = pl.program_id(0)
    last = pl.num_programs(0) - 1
    cur = i % 2

    @pl.when(i == 0)
    def _():
        acc_buf[1] = jnp.zeros(acc_buf.shape[1:], F32)

    def norm_slice(s):
        rows = slice(s * rs, (s + 1) * rs)
        y = _layer_norm(acc_buf[1 - cur, rows, :], lng_ref[...], lnb_ref[...])
        y_ref[rows, :] = y
        bits = lax.bitcast_convert_type(y, jnp.int32)
        lanes = bits[:, 0:128]
        for l in range(1, bits.shape[1] // 128):
            lanes = lanes | bits[:, l * 128:(l + 1) * 128]
        word = lanes[0:SUBLANES, :]
        for r in range(1, rs // SUBLANES):
            word = word | lanes[r * SUBLANES:(r + 1) * SUBLANES, :]
        return lax.bitcast_convert_type(word & zero_ref[...], F32)

    def down(c, dep=None):
        cols = slice(c * FFN_CHUNK, (c + 1) * FFN_CHUNK)
        xb = xb_buf[...]
        g = _dot(xb, wg_ref[:, cols])
        u = _dot(xb, wu_ref[:, cols])
        hid = g * _sigmoid(g) * u
        if dep is not None:
            corner = jnp.concatenate([hid[0:SUBLANES, 0:128] + dep, hid[0:SUBLANES, 128:]], axis=1)
            hid = jnp.concatenate([corner, hid[SUBLANES:, :]], axis=0)
        return _dot(hid.astype(BF16), wd_ref[cols, :])

    @pl.when(i < last)
    def _():
        xb_buf[...] = x_ref[...].astype(BF16)
        acc_buf[cur] = alpha * x_ref[...] + down(0)
        for c in range(1, n_chunks):
            dep = norm_slice(c - 1) if c - 1 < FFN_NORM_SLICES else None
            acc_buf[cur] += down(c, dep)

    @pl.when(i == last)
    def _():
        y_ref[...] = _layer_norm(acc_buf[1 - cur], lng_ref[...], lnb_ref[...])


def _ffn_call(x2d, layer, wg, wu, wd, ln_g, ln_b, alpha, tm):
    rows, d_model = x2d.shape
    assert rows % tm == 0 and wg.shape[2] % FFN_CHUNK == 0
    assert wg.shape[2] // FFN_CHUNK > FFN_NORM_SLICES and tm % (FFN_NORM_SLICES * SUBLANES) == 0
    n_tiles = rows // tm
    zero = jnp.zeros((SUBLANES, 128), jnp.int32)
    consts = (wg, wu, wd, ln_g, ln_b)

    def const_spec(arr):
        zeros = (0,) * (arr.ndim - 1)
        return pl.BlockSpec((None,) + arr.shape[1:], lambda i: (layer,) + zeros, pipeline_mode=pl.Buffered(1))

    return pl.pallas_call(
        functools.partial(_ffn_body, alpha),
        grid=(n_tiles + 1,),
        in_specs=[pl.BlockSpec((tm, d_model), lambda i: (jnp.minimum(i, n_tiles - 1), 0)),
                  pl.BlockSpec(zero.shape, lambda i: (0, 0))] + [const_spec(c) for c in consts],
        out_specs=pl.BlockSpec((tm, d_model), lambda i: (jnp.maximum(i - 1, 0), 0)),
        out_shape=jax.ShapeDtypeStruct((rows, d_model), F32),
        scratch_shapes=[pltpu.VMEM((tm, d_model), BF16), pltpu.VMEM((2, tm, d_model), F32)],
        compiler_params=pltpu.CompilerParams(
            dimension_semantics=("arbitrary",), vmem_limit_bytes=VMEM_LIMIT_BYTES),
        name="ffn",
    )(x2d, zero, *consts)


def _ffn(x, layer, ffn_prm, alpha, tm):
    d_model = x.shape[-1]
    return _ffn_call(x.reshape(-1, d_model), layer, *ffn_prm, alpha, tm).reshape(x.shape)


def _row(v):
    return v.reshape(1, -1).astype(F32)


def _gate_tiles(ga_w, gx_w):
    heads, blk, _ = ga_w.shape
    per_tile = MXU_DIM // blk
    tiles = []
    for j in range(heads // per_tile):
        def diag(w):
            return jax.scipy.linalg.block_diag(*[w[j * per_tile + i] for i in range(per_tile)])
        tiles.append(jnp.concatenate([diag(ga_w), diag(gx_w)], axis=1))
    return jnp.stack(tiles).astype(BF16)


def kernel(x_prompt, x_sample, state_rec_h, state_rec_conv, state_cf_conv, cache_k, cache_v, w_in_ab, rec_conv_w, rec_conv_b, rec_gate_a_w, rec_gate_a_b, rec_gate_x_w, rec_gate_x_b, rec_lambda, cf_conv_w, cf_conv_b, cf_norm_g, cf_norm_b, w_out_ab, w_qkv, attn_sinks, w_out_c, ln_mix_g, ln_mix_b, w_ff_gate, w_ff_up, w_ff_down, ln_ff_g, ln_ff_b):
    depth = ln_mix_g.shape[0]
    alpha = (2 * depth) ** 0.25
    bp, seq_p, d_model = x_prompt.shape
    bs, seq_s, _ = x_sample.shape
    d_rec = rec_conv_w.shape[-1]
    d_conv = cf_conv_w.shape[-1]
    d_kv = N_KV * HEAD_DIM
    tt_ab = min(seq_p, 64)
    tt_c = min(seq_p, 1024)
    tt_s = seq_s
    tm_p = min(bp * seq_p, 1024)
    tm_s = bs * seq_s

    ffn_prm = (w_ff_gate.astype(BF16), w_ff_up.astype(BF16), w_ff_down.astype(BF16),
               ln_ff_g.reshape(depth, 1, d_model).astype(F32), ln_ff_b.reshape(depth, 1, d_model).astype(F32))

    yp, ys = x_prompt, x_sample
    p_h, s_h, p_rc, s_rc, p_cf, s_cf, p_k, s_k, p_v, s_v = ([] for _ in range(10))
    for layer in range(depth):
        j = layer // 2
        ln_g, ln_b = _row(ln_mix_g[layer]), _row(ln_mix_b[layer])
        if layer % 2 == 0:
            prm =(w_in_ab[j].astype(BF16), rec_conv_w[j], _row(rec_conv_b[j]),
                   _gate_tiles(rec_gate_a_w[j], rec_gate_x_w[j]), _row(rec_gate_a_b[j]), _row(rec_gate_x_b[j]),
                   _row(rec_lambda[j]), cf_conv_w[j], _row(cf_conv_b[j]), _row(cf_norm_g[j]), _row(cf_norm_b[j]),
                   w_out_ab[j].astype(BF16))
            yp, rc, hl, cf = _mixer_ab_call(
                yp, jnp.zeros((bp, REC_CONV - 1, d_rec), F32), jnp.zeros((bp, d_rec), F32),
                jnp.zeros((bp, CONV_WIDTH - 1, d_conv), F32), prm, ln_g, ln_b, alpha, tt_ab)
            p_rc.append(rc), p_h.append(hl), p_cf.append(cf)
            ys, rc, hl, cf = _mixer_ab_call(
                ys, state_rec_conv[j], state_rec_h[j], state_cf_conv[j], prm, ln_g, ln_b, alpha, tt_s)
            s_rc.append(rc), s_h.append(hl), s_cf.append(cf)
        else:
            wqkv, w_out = w_qkv[j].astype(BF16), w_out_c[j].astype(BF16)
            sinks = attn_sinks[j].astype(F32)
            d_q = N_HEADS * HEAD_DIM
            wq_t = wqkv[:, 0:d_q].T
            wk, wv = wqkv[:, d_q:d_q + d_kv], wqkv[:, d_q + d_kv:d_q + 2 * d_kv]
            yp, kk, vv = _mixer_c_prompt_call(yp, wq_t, wk, wv, wv.T, sinks, w_out, ln_g, ln_b, alpha, tt_c)
            p_k.append(kk.reshape(bp, WINDOW, N_KV, HEAD_DIM)), p_v.append(vv.reshape(bp, WINDOW, N_KV, HEAD_DIM))
            ck = cache_k[j].reshape(bs, WINDOW, d_kv)
            cv = cache_v[j].reshape(bs, WINDOW, d_kv)
            ys, kk, vv = _mixer_c_call(ys, ck, cv, wqkv, sinks, w_out, ln_g, ln_b, alpha)
            s_k.append(kk.reshape(bs, WINDOW, N_KV, HEAD_DIM)), s_v.append(vv.reshape(bs, WINDOW, N_KV, HEAD_DIM))
        yp = _ffn(yp, layer, ffn_prm, alpha, tm_p)
        ys = _ffn(ys, layer, ffn_prm, alpha, tm_s)
    return (yp, ys, jnp.stack(p_h), jnp.stack(s_h), jnp.stack(p_rc), jnp.stack(s_rc),
            jnp.stack(p_cf), jnp.stack(s_cf), jnp.stack(p_k), jnp.stack(s_k), jnp.stack(p_v), jnp.stack(s_v))
```
